```python
import jax, jax.numpy as jnp
from jax import lax
import numpy as np


D_MODEL = 4096
BATCH = 8
SEQ = 2048
DEPTH = 1
DEC_BATCH = 8
DEC_SEQ = 64
PAST_LEN = 1024

CHUNK = 64
Q_BLOCK = 128
PLE_DIM = 256
EPS = 1e-6
SB_HEADS = 16
SB_HEAD_DIM = 128
D_SB = SB_HEADS * SB_HEAD_DIM
SSD_HEADS = 32
SSD_HEAD_DIM = 64
D_SSD = SSD_HEADS * SSD_HEAD_DIM
SSD_GROUPS = 8
SSD_STATE = 128
SSD_HPG = SSD_HEADS // SSD_GROUPS
CONV_WIDTH = 4
CONV_CH = D_SSD + 2 * SSD_GROUPS * SSD_STATE
D_MIX = D_SB + D_SSD
N_IN = 4 * D_SB + CONV_CH + D_SSD + SSD_HEADS
_SPLITS = (D_SB, 2 * D_SB, 3 * D_SB, 4 * D_SB, 4 * D_SB + CONV_CH, 4 * D_SB + CONV_CH + D_SSD)

kernel_name = 'hymba_stickbreak_ssd_stream_step'


def rmsnorm(x, w):
    xf = x.astype(jnp.float32)
    y = xf * lax.rsqrt(jnp.mean(xf * xf, axis=-1, keepdims=True) + EPS)
    return (y * w.astype(jnp.float32)).astype(x.dtype)


def stick_breaking(q, k_all, v_all, q_pos, k_pos):
    z = jnp.einsum('bqhd,bkhd->bhqk', q.astype(jnp.float32), k_all.astype(jnp.float32)) * (SB_HEAD_DIM ** -0.5)
    mask = k_pos[None, :] < q_pos[:, None]
    log_stay = jnp.where(mask, jax.nn.log_sigmoid(-z), 0.0)
    later = lax.cumsum(log_stay, axis=3, reverse=True) - log_stay
    att = jnp.where(mask, jnp.exp(jax.nn.log_sigmoid(z) + later), 0.0)
    return jnp.einsum('bhqk,bkhd->bqhd', att, v_all.astype(jnp.float32))


def ssd_chunked(x, dt, a, bm, cm, h0):
    bsz, t = x.shape[0], x.shape[1]
    nc = -(-t // CHUNK)
    pad = nc * CHUNK - t
    padt = lambda arr: jnp.pad(arr, [(0, 0), (0, pad)] + [(0, 0)] * (arr.ndim - 2))
    x, dt, bm, cm = padt(x), padt(dt), padt(bm), padt(cm)
    L = CHUNK
    xc = x.reshape(bsz, nc, L, SSD_GROUPS, SSD_HPG, SSD_HEAD_DIM)
    dtc = dt.reshape(bsz, nc, L, SSD_GROUPS, SSD_HPG)
    bc = bm.reshape(bsz, nc, L, SSD_GROUPS, SSD_STATE)
    cc = cm.reshape(bsz, nc, L, SSD_GROUPS, SSD_STATE)
    acum = jnp.cumsum(dtc * a.reshape(SSD_GROUPS, SSD_HPG), axis=2)
    diff = acum[:, :, :, None] - acum[:, :, None, :]
    causal = jnp.tril(jnp.ones((L, L), dtype=bool))[:, :, None, None]
    decay = jnp.exp(jnp.where(causal, diff, -jnp.inf))
    cb = jnp.einsum('bclgn,bcsgn->bclsg', cc, bc)
    wts = cb[..., None] * decay * dtc[:, :, None]
    y_intra = jnp.einsum('bclsgh,bcsghp->bclghp', wts, xc)
    decay_end = jnp.exp(acum[:, :, -1:] - acum) * dtc
    states = jnp.einsum('bclgh,bclgn,bclghp->bcghpn', decay_end, bc, xc)
    block_decay = jnp.exp(acum[:, :, -1])

    def step(h, inp):
        st, bd, c_blk, ac = inp
        y_in = jnp.einsum('blgn,bghpn->blghp', c_blk, h) * jnp.exp(ac)[..., None]
        h = bd[..., None, None] * h + st
        return h, y_in

    h_init = h0.reshape(bsz, SSD_GROUPS, SSD_HPG, SSD_HEAD_DIM, SSD_STATE)
    h_fin, y_inter = lax.scan(step, h_init, (jnp.moveaxis(states, 1, 0), jnp.moveaxis(block_decay, 1, 0),
                                             jnp.moveaxis(cc, 1, 0), jnp.moveaxis(acum, 1, 0)))
    y = (y_intra + jnp.moveaxis(y_inter, 0, 1)).reshape(bsz, nc * L, SSD_HEADS, SSD_HEAD_DIM)[:, :t]
    return y, h_fin.reshape(bsz, SSD_HEADS, SSD_HEAD_DIM, SSD_STATE)


def hybrid_mixer(h, k_past, v_past, conv_past, ssd_past, w_in, conv_w, conv_b, dt_bias, a_log, d_skip,
                 sb_norm, ssd_norm, w_out):
    bsz, t, _ = h.shape
    n_past = k_past.shape[1]
    proj = h @ w_in
    q, k, v, g_sb, xbc, z_ssd, dt_raw = jnp.split(proj, _SPLITS, axis=-1)
    q = q.reshape(bsz, t, SB_HEADS, SB_HEAD_DIM)
    k = k.reshape(bsz, t, SB_HEADS, SB_HEAD_DIM)
    v = v.reshape(bsz, t, SB_HEADS, SB_HEAD_DIM)
    k_all = jnp.concatenate([k_past.astype(k.dtype), k], axis=1)
    v_all = jnp.concatenate([v_past.astype(v.dtype), v], axis=1)
    k_pos = jnp.arange(n_past + t)
    qb = min(Q_BLOCK, t)
    nblk = -(-t // qb)
    qpad = nblk * qb - t
    q_blocks = jnp.pad(q, ((0, 0), (0, qpad), (0, 0), (0, 0))).reshape(
        bsz, nblk, qb, SB_HEADS, SB_HEAD_DIM).swapaxes(0, 1)
    q_pos = (n_past + jnp.arange(nblk * qb)).reshape(nblk, qb)
    o = lax.map(lambda blk: stick_breaking(blk[0], k_all, v_all, blk[1], k_pos), (q_blocks, q_pos))
    o = o.swapaxes(0, 1).reshape(bsz, nblk * qb, D_SB)[:, :t].astype(h.dtype)
    o_sb = rmsnorm(o * jax.nn.silu(g_sb), sb_norm)
    xbc_ext = jnp.concatenate([conv_past.astype(xbc.dtype), xbc], axis=1)
    conv = conv_b
    for j in range(CONV_WIDTH):
        conv = conv + conv_w[j] * xbc_ext[:, j:j + t]
    xbc_act = jax.nn.silu(conv)
    x_s, b_s, c_s = jnp.split(xbc_act, [D_SSD, D_SSD + SSD_GROUPS * SSD_STATE], axis=-1)
    dt = jax.nn.softplus(dt_raw.astype(jnp.float32) + dt_bias.astype(jnp.float32))
    a = -jnp.exp(a_log.astype(jnp.float32))
    x_h = x_s.astype(jnp.float32).reshape(bsz, t, SSD_HEADS, SSD_HEAD_DIM)
    y, ssd_new = ssd_chunked(x_h, dt, a,
                             b_s.astype(jnp.float32).reshape(bsz, t, SSD_GROUPS, SSD_STATE),
                             c_s.astype(jnp.float32).reshape(bsz, t, SSD_GROUPS, SSD_STATE),
                             ssd_past.astype(jnp.float32))
    y = (y + d_skip.astype(jnp.float32)[:, None] * x_h).reshape(bsz, t, D_SSD).astype(h.dtype)
    o_ssd = rmsnorm(y * jax.nn.silu(z_ssd), ssd_norm)
    out = jnp.concatenate([o_sb, o_ssd], axis=-1) @ w_out
    new_conv = xbc_ext[:, -(CONV_WIDTH - 1):]
    return out, k, v, new_conv, ssd_new.astype(h.dtype)


def trunk_layer(x, p, k_past, v_past, conv_past, ssd_past, norm_pre, norm_post, w_in, conv_w, conv_b,
                dt_bias, a_log, d_skip, sb_norm, ssd_norm, w_out, w_ple_gate, w_ple_proj, ple_norm):
    h = rmsnorm(x, norm_pre)
    m, k_new, v_new, conv_new, ssd_new = hybrid_mixer(h, k_past, v_past, conv_past, ssd_past, w_in, conv_w,
                                                      conv_b, dt_bias, a_log, d_skip, sb_norm, ssd_norm, w_out)
    x = x + rmsnorm(m.astype(x.dtype), norm_post)
    e = p.astype(x.dtype) @ w_ple_proj
    gate = jax.nn.sigmoid(x @ w_ple_gate)
    x = x + rmsnorm(gate * e, ple_norm)
    return x, k_new, v_new, conv_new, ssd_new


def setup_inputs(seed: int = 0) -> dict:
    key = jax.random.key(seed)
    ks = jax.random.split(key, 24)
    f32 = jnp.float32
    nrm = lambda k, shape, s=1.0: jax.random.normal(k, shape, f32) * s
    gain = lambda k, n: 1.0 + 0.01 * jax.random.normal(k, (DEPTH, n), f32)
    dt0 = jnp.exp(jax.random.uniform(ks[13], (DEPTH, SSD_HEADS), f32) * (jnp.log(0.1) - jnp.log(0.001)) + jnp.log(0.001))
    return {
        'x_prompt': nrm(ks[0], (BATCH, SEQ, D_MODEL)),
        'x_sample': nrm(ks[1], (DEC_BATCH, DEC_SEQ, D_MODEL)),
        'cache_k': nrm(ks[2], (DEPTH, DEC_BATCH, PAST_LEN, SB_HEADS, SB_HEAD_DIM)),
        'cache_v': nrm(ks[3], (DEPTH, DEC_BATCH, PAST_LEN, SB_HEADS, SB_HEAD_DIM)),
        'state_conv': nrm(ks[4], (DEPTH, DEC_BATCH, CONV_WIDTH - 1, CONV_CH)),
        'state_ssd': nrm(ks[5], (DEPTH, DEC_BATCH, SSD_HEADS, SSD_HEAD_DIM, SSD_STATE), 0.5),
        'p_prompt': nrm(ks[6], (DEPTH, BATCH, SEQ, PLE_DIM)),
        'p_sample': nrm(ks[7], (DEPTH, DEC_BATCH, DEC_SEQ, PLE_DIM)),
        'norm_pre': gain(ks[8], D_MODEL),
        'norm_post': gain(ks[9], D_MODEL),
        'w_in': nrm(ks[10], (DEPTH, D_MODEL, N_IN), D_MODEL ** -0.5),
        'conv_w': nrm(ks[11], (DEPTH, CONV_WIDTH, CONV_CH), CONV_WIDTH ** -0.5),
        'conv_b': nrm(ks[12], (DEPTH, CONV_CH), 0.01),
        'dt_bias': dt0 + jnp.log(-jnp.expm1(-dt0)),
        'a_log': jnp.log(jax.random.uniform(ks[14], (DEPTH, SSD_HEADS), f32, 1.0, 16.0)),
        'd_skip': gain(ks[15], SSD_HEADS),
        'sb_norm': gain(ks[16], D_SB),
        'ssd_norm': gain(ks[17], D_SSD),
        'w_out': nrm(ks[18], (DEPTH, D_MIX, D_MODEL), D_MIX ** -0.5),
        'w_ple_gate': nrm(ks[19], (DEPTH, D_MODEL, D_MODEL), D_MODEL ** -0.5),
        'w_ple_proj': nrm(ks[20], (DEPTH, PLE_DIM, D_MODEL), PLE_DIM ** -0.5),
        'ple_norm': gain(ks[21], D_MODEL),
    }


def reference(x_prompt, x_sample, cache_k, cache_v, state_conv, state_ssd, p_prompt, p_sample,
              norm_pre, norm_post, w_in, conv_w, conv_b, dt_bias, a_log, d_skip, sb_norm, ssd_norm,
              w_out, w_ple_gate, w_ple_proj, ple_norm):
    yp, ys = x_prompt, x_sample
    kp_l, vp_l, cp_l, sp_l, ks_l, vs_l, cs_l, ss_l = [], [], [], [], [], [], [], []
    for i in range(DEPTH):
        lw = (norm_pre[i], norm_post[i], w_in[i], conv_w[i], conv_b[i], dt_bias[i], a_log[i], d_skip[i],
              sb_norm[i], ssd_norm[i], w_out[i], w_ple_gate[i], w_ple_proj[i], ple_norm[i])
        dt_p = x_prompt.dtype
        yp, kp, vp, cp, sp = trunk_layer(
            yp, p_prompt[i],
            jnp.zeros((BATCH, 0, SB_HEADS, SB_HEAD_DIM), dt_p),
            jnp.zeros((BATCH, 0, SB_HEADS, SB_HEAD_DIM), dt_p),
            jnp.zeros((BATCH, CONV_WIDTH - 1, CONV_CH), dt_p),
            jnp.zeros((BATCH, SSD_HEADS, SSD_HEAD_DIM, SSD_STATE), dt_p), *lw)
        ys, ks, vs, cs, ss = trunk_layer(ys, p_sample[i], cache_k[i], cache_v[i], state_conv[i], state_ssd[i], *lw)
        kp_l.append(kp); vp_l.append(vp); cp_l.append(cp); sp_l.append(sp)
        ks_l.append(ks); vs_l.append(vs); cs_l.append(cs); ss_l.append(ss)
    k_prompt, v_prompt = jnp.stack(kp_l), jnp.stack(vp_l)
    conv_prompt, ssd_prompt = jnp.stack(cp_l), jnp.stack(sp_l)
    k_sample, v_sample = jnp.stack(ks_l), jnp.stack(vs_l)
    conv_sample, ssd_sample = jnp.stack(cs_l), jnp.stack(ss_l)
    return (yp, ys, k_prompt, v_prompt, conv_prompt, ssd_prompt, k_sample, v_sample, conv_sample, ssd_sample)
```

```python
import functools
import math

import jax
import jax.numpy as jnp
from jax import lax
from jax.experimental import pallas as pl
from jax.experimental.pallas import tpu as pltpu

F32 = jnp.float32
BF16 = jnp.bfloat16

EPS = 1e-6
SB_HEAD_DIM = 128
SSD_HEAD_DIM = 64
SSD_STATE = 128
SSD_GROUPS = 8
CONV_WIDTH = 4
LANES = 128
SUBLANES = 8
V7X_VMEM_BYTES = 64 * 2**20


def _vmem_limit(block_bytes, scratch_bytes, temp_bytes):
    need = 2 * block_bytes + scratch_bytes + temp_bytes
    return int(min(need, V7X_VMEM_BYTES - 2 * 2**20))


def _nbytes(shape, dtype):
    return math.prod(shape) * jnp.dtype(dtype).itemsize


def _sigmoid(x):
    return 1.0 / (1.0 + jnp.exp(-x))


def _silu(x):
    return x * _sigmoid(x)


def _split_bf16(x, parts):
    out = []
    r = x
    for p in range(parts):
        h = r.astype(BF16)
        out.append(h)
        if p + 1 < parts:
            r = r - h.astype(F32)
    return out


def _dot(a, b):
    return jnp.dot(a, b, preferred_element_type=F32)


def _dot_split(x, m, parts):
    acc = None
    for h in _split_bf16(x, parts):
        t = _dot(h, m)
        acc = t if acc is None else acc + t
    return acc


def _in_proj_kernel(x_ref, nw_ref, w_ref, wdt_ref, q_ref, k_ref, v_ref, g_ref, xbc_ref, z_ref, dt_ref, h_ref,
                    *, bounds, q_scale, row_chunk):
    j = pl.program_id(1)
    tm = x_ref.shape[0]

    @pl.when(j == 0)
    def _norm():
        def body(r, carry):
            rows = pl.ds(pl.multiple_of(r * row_chunk, row_chunk), row_chunk)
            xf = x_ref[rows, :]
            ms = jnp.mean(xf * xf, axis=-1, keepdims=True)
            h_ref[rows, :] = (xf * lax.rsqrt(ms + EPS) * nw_ref[...]).astype(BF16)
            return carry

        lax.fori_loop(0, tm // row_chunk, body, 0)

    def segment(lo, hi, out_ref, scale):
        @pl.when((j >= lo) & (j < hi))
        def _():
            acc = _dot(h_ref[...], w_ref[...])
            if scale is not None:
                acc = acc * scale
            out_ref[...] = acc.astype(out_ref.dtype)

    outs = (q_ref, k_ref, v_ref, g_ref, xbc_ref, z_ref)
    for s, out_ref in enumerate(outs):
        segment(bounds[s], bounds[s + 1], out_ref, q_scale if s == 0 else None)

    @pl.when(j == bounds[-1])
    def _dt():
        dt_ref[...] = _dot(h_ref[...], wdt_ref[...])


def _in_proj(x, norm_w, w_main, w_dt, *, d_sb, conv_ch, d_ssd, tm):
    m, d = x.shape
    tn = math.gcd(math.gcd(d_sb, 512), math.gcd(conv_ch, d_ssd))
    widths = (d_sb, d_sb, d_sb, d_sb, conv_ch, d_ssd)
    bounds = [0]
    for w in widths:
        bounds.append(bounds[-1] + w // tn)
    bounds = tuple(bounds)
    n_tiles = bounds[-1]
    assert w_main.shape == (d, n_tiles * tn) and w_dt.shape == (d, LANES) and m % tm == 0

    def out_map(lo, hi):
        return lambda i, j: (i, jnp.clip(j - lo, 0, hi - lo - 1))

    out_shapes = [jax.ShapeDtypeStruct((m, d_sb), BF16)]
    out_shapes += [jax.ShapeDtypeStruct((m, wd), F32) for wd in widths[1:]]
    out_shapes.append(jax.ShapeDtypeStruct((m, LANES), F32))
    out_specs = [pl.BlockSpec((tm, tn), out_map(bounds[s], bounds[s + 1])) for s in range(len(widths))]
    out_specs.append(pl.BlockSpec((tm, LANES), lambda i, j: (i, 0)))

    block_bytes = (_nbytes((tm, d), F32) + _nbytes((d, tn), BF16) + _nbytes((d, LANES), BF16)
                   + 6 * _nbytes((tm, tn), F32) + _nbytes((tm, LANES), F32))
    scratch_bytes = _nbytes((tm, d), BF16)
    temp_bytes = 2 * _nbytes((tm, tn), F32)
    kern = functools.partial(_in_proj_kernel, bounds=bounds, q_scale=SB_HEAD_DIM ** -0.5, row_chunk=16)
    return pl.pallas_call(
        kern,
        grid=(m // tm, n_tiles + 1),
        in_specs=[
            pl.BlockSpec((tm, d), lambda i, j: (i, 0)),
            pl.BlockSpec((1, d), lambda i, j: (0, 0)),
            pl.BlockSpec((d, tn), lambda i, j: (0, jnp.minimum(j, n_tiles - 1))),
            pl.BlockSpec((d, LANES), lambda i, j: (0, 0)),
        ],
        out_specs=out_specs,
        out_shape=out_shapes,
        scratch_shapes=[pltpu.VMEM((tm, d), BF16)],
        compiler_params=pltpu.CompilerParams(
            dimension_semantics=("arbitrary", "arbitrary"),
            vmem_limit_bytes=_vmem_limit(block_bytes, scratch_bytes, temp_bytes)),
        name="in_proj",
    )(x, norm_w, w_main, w_dt)


def _attn_kernel(*refs, tq, tp, n_past):
    if n_past:
        q_ref, k_ref, v_ref, kp_ref, vp_ref, u_ref, o_ref, acc_ref, carry_ref = refs
    else:
        q_ref, k_ref, v_ref, u_ref, o_ref, acc_ref, carry_ref = refs
    qi = pl.program_id(2)
    q = q_ref[...]
    acc_ref[...] = jnp.zeros_like(acc_ref)
    carry_ref[...] = jnp.zeros_like(carry_ref)

    def block(kb, vb, u, mask):
        z = lax.dot_general(q, kb.astype(BF16), (((1,), (1,)), ((), ())), preferred_element_type=F32)
        sp = jnp.log(1.0 + jnp.exp(-jnp.abs(z)))
        log_stop = jnp.minimum(z, 0.0) - sp
        log_stay = log_stop - z
        if mask is not None:
            log_stay = jnp.where(mask, log_stay, 0.0)
        later = _dot_split(log_stay, u, 2)
        carry = carry_ref[...]
        att = jnp.exp(log_stop + later + carry)
        if mask is not None:
            att = jnp.where(mask, att, 0.0)
        acc_ref[...] += _dot(att.astype(BF16), vb.astype(BF16))
        carry_ref[...] = carry + jnp.sum(log_stay, axis=-1, keepdims=True)

    row = lax.broadcasted_iota(jnp.int32, (tq, tq), 0)
    col = lax.broadcasted_iota(jnp.int32, (tq, tq), 1)
    u_new = u_ref[:tq, :tq]
    diag = pl.ds(pl.multiple_of(qi * tq, tq), tq)
    block(k_ref[diag, :], v_ref[diag, :], u_new, col < row)

    def new_body(i, c):
        rows = pl.ds(pl.multiple_of((qi - 1 - i) * tq, tq), tq)
        block(k_ref[rows, :], v_ref[rows, :], u_new, None)
        return c

    lax.fori_loop(0, qi, new_body, 0)

    if n_past:
        u_past = u_ref[:tp, :tp]

        def past_body(i, c):
            rows = pl.ds(pl.multiple_of((n_past - 1 - i) * tp, tp), tp)
            block(kp_ref[rows, :], vp_ref[rows, :], u_past, None)
            return c

        lax.fori_loop(0, n_past, past_body, 0)

    o_ref[...] = acc_ref[...]


def _attention(q, k, v, k_past, v_past, *, tq, tp):
    b, t, hd = k.shape
    d = SB_HEAD_DIM
    heads = hd // d
    p = 0 if k_past is None else k_past.shape[1]
    assert t % tq == 0 and p % tp == 0
    n_past = p // tp
    tu = max(tq, tp) if n_past else tq
    ridx = lax.broadcasted_iota(jnp.int32, (tu, tu), 0)
    cidx = lax.broadcasted_iota(jnp.int32, (tu, tu), 1)
    u = (ridx > cidx).astype(BF16)

    q_spec = pl.BlockSpec((None, tq, d), lambda bi, h, qi: (bi, qi, h))
    kv_spec = pl.BlockSpec((None, t, d), lambda bi, h, qi: (bi, 0, h))
    in_specs = [q_spec, kv_spec, kv_spec]
    args = [q, k, v]
    block_bytes = _nbytes((tq, d), BF16) + 2 * _nbytes((t, d), F32) + _nbytes((tu, tu), BF16) + _nbytes((tq, d), F32)
    if n_past:
        past_spec = pl.BlockSpec((None, p, d), lambda bi, h, qi: (bi, 0, h))
        in_specs += [past_spec, past_spec]
        args += [k_past, v_past]
        block_bytes += 2 * _nbytes((p, d), F32)
    in_specs.append(pl.BlockSpec((tu, tu), lambda bi, h, qi: (0, 0)))
    args.append(u)
    scratch_bytes = _nbytes((tq, d), F32) + _nbytes((tq, LANES), F32)
    temp_bytes = 8 * _nbytes((tq, tu), F32)
    kern = functools.partial(_attn_kernel, tq=tq, tp=tp, n_past=n_past)
    return pl.pallas_call(
        kern,
        grid=(b, heads, t // tq),
        in_specs=in_specs,
        out_specs=pl.BlockSpec((None, tq, d), lambda bi, h, qi: (bi, qi, h)),
        out_shape=jax.ShapeDtypeStruct((b, t, hd), F32),
        scratch_shapes=[pltpu.VMEM((tq, d), F32), pltpu.VMEM((tq, 1), F32)],
        compiler_params=pltpu.CompilerParams(
            dimension_semantics=("arbitrary", "arbitrary", "arbitrary"),
            vmem_limit_bytes=_vmem_limit(block_bytes, scratch_bytes, temp_bytes)),
        name="stick_breaking",
    )(*args)


def _ssd_kernel(xbc_ref, dt_ref, convp_ref, statep_ref, cw_ref, cb_ref, dtb_ref, alog_ref, dskip_ref, e_ref, tri_ref,
                y_ref, state_ref, ext_ref, act_ref, ht_ref, *, chunk, d_ssd, col_slab):
    c = pl.program_id(1)
    n_chunks = pl.num_programs(1)
    ch = xbc_ref.shape[-1]
    gn = SSD_GROUPS * SSD_STATE
    hpg = d_ssd // SSD_HEAD_DIM // SSD_GROUPS
    gw = hpg * SSD_HEAD_DIM
    pad = SUBLANES

    @pl.when(c == 0)
    def _init():
        ext_ref[0:pad, :] = jnp.zeros((pad, ch), F32)
        ext_ref[pad - (CONV_WIDTH - 1):pad, :] = convp_ref[...]
        for g in range(SSD_GROUPS):
            ht_ref[g] = statep_ref[g].T

    @pl.when(c > 0)
    def _shift():
        ext_ref[0:pad, :] = ext_ref[chunk:chunk + pad, :]

    ext_ref[pad:pad + chunk, :] = xbc_ref[...]

    for s in range(ch // col_slab):
        cols = slice(s * col_slab, (s + 1) * col_slab)
        acc = cb_ref[:, cols]
        for jj in range(CONV_WIDTH):
            lo = pad - (CONV_WIDTH - 1) + jj
            acc = acc + cw_ref[jj:jj + 1, cols] * ext_ref[lo:lo + chunk, cols]
        act_ref[:, cols] = _silu(acc)

    dt_in = dt_ref[...] + dtb_ref[...]
    dt = jnp.maximum(dt_in, 0.0) + jnp.log(1.0 + jnp.exp(-jnp.abs(dt_in)))
    da = dt * (-jnp.exp(alog_ref[...]))
    tri = tri_ref[...]
    acum = _dot_tri(tri, da)
    last = acum[chunk - 1:chunk, :]
    acum_t = acum.T
    dt_t = dt.T
    row = lax.broadcasted_iota(jnp.int32, (chunk, chunk), 0)
    col = lax.broadcasted_iota(jnp.int32, (chunk, chunk), 1)
    causal = col <= row

    e = e_ref[...]
    de_e = _dot_split(jnp.exp(last - acum) * dt, e, 2)
    ea_e = _dot_split(jnp.exp(acum), e, 2)
    bd_e = _dot_split(jnp.broadcast_to(jnp.exp(last), (SUBLANES, LANES)), e, 2)[0:1, :]

    lane = lax.broadcasted_iota(jnp.int32, (chunk, gw), 1)
    for g in range(SSD_GROUPS):
        gs = slice(g * gw, (g + 1) * gw)
        xg = act_ref[:, gs]
        bg = act_ref[:, d_ssd + g * SSD_STATE:d_ssd + (g + 1) * SSD_STATE].astype(BF16)
        cg = act_ref[:, d_ssd + gn + g * SSD_STATE:d_ssd + gn + (g + 1) * SSD_STATE].astype(BF16)
        cb = lax.dot_general(cg, bg, (((1,), (1,)), ((), ())), preferred_element_type=F32)
        xgb = xg.astype(BF16)
        y_intra = None
        for hh in range(hpg):
            h = g * hpg + hh
            diff = acum[:, h:h + 1] - acum_t[h:h + 1, :]
            decay = jnp.exp(jnp.where(causal, diff, -jnp.inf))
            w = (cb * decay * dt_t[h:h + 1, :]).astype(BF16)
            in_head = (lane >= hh * SSD_HEAD_DIM) & (lane < (hh + 1) * SSD_HEAD_DIM)
            t = _dot(w, jnp.where(in_head, xgb, jnp.zeros_like(xgb)))
            y_intra = t if y_intra is None else y_intra + t
        h_prev = ht_ref[g]
        y_inter = _dot(cg, h_prev.astype(BF16)) * ea_e[:, gs]
        xw = (xg * de_e[:, gs]).astype(BF16)
        s_t = lax.dot_general(bg, xw, (((0,), (0,)), ((), ())), preferred_element_type=F32)
        ht_ref[g] = h_prev * bd_e[:, gs] + s_t
        y_ref[:, gs] = y_intra + y_inter + dskip_ref[:, gs] * xg

    @pl.when(c == n_chunks - 1)
    def _final():
        for g in range(SSD_GROUPS):
            state_ref[g] = ht_ref[g].T


def _dot_tri(tri, x):
    acc = None
    for h in _split_bf16(x, 3):
        t = _dot(tri, h)
        acc = t if acc is None else acc + t
    return acc


def _ssd(xbc, dt_raw, conv_past, state_past, conv_w, conv_b, dt_bias, a_log, d_skip, *, d_ssd, chunk):
    b, t, ch = xbc.shape
    heads = d_ssd // SSD_HEAD_DIM
    hpg = heads // SSD_GROUPS
    gw = hpg * SSD_HEAD_DIM
    assert t % chunk == 0 and heads <= LANES
    n_chunks = t // chunk
    pad_h = LANES - heads
    dtb = jnp.pad(dt_bias.astype(F32), (0, pad_h)).reshape(1, LANES)
    alog = jnp.pad(a_log.astype(F32), (0, pad_h)).reshape(1, LANES)
    dskip = jnp.repeat(d_skip.astype(F32), SSD_HEAD_DIM).reshape(1, d_ssd)
    hidx = lax.broadcasted_iota(jnp.int32, (LANES, d_ssd), 0)
    lidx = lax.broadcasted_iota(jnp.int32, (LANES, d_ssd), 1)
    expand = (lidx // SSD_HEAD_DIM == hidx).astype(BF16)
    ridx = lax.broadcasted_iota(jnp.int32, (chunk, chunk), 0)
    cidx = lax.broadcasted_iota(jnp.int32, (chunk, chunk), 1)
    tri = (cidx <= ridx).astype(BF16)
    state_g = state_past.reshape(b, SSD_GROUPS, gw, SSD_STATE)

    full = lambda shape: pl.BlockSpec(shape, lambda bi, c: (0,) * len(shape))
    col_slab = 512
    block_bytes = (_nbytes((chunk, ch), F32) + _nbytes((chunk, LANES), F32) + _nbytes((SUBLANES, ch), F32)
                   + 2 * _nbytes((SSD_GROUPS, gw, SSD_STATE), F32) + _nbytes((CONV_WIDTH + 1, ch), F32) * 2
                   + _nbytes((LANES, d_ssd), BF16) + _nbytes((chunk, chunk), BF16) + _nbytes((chunk, d_ssd), F32))
    scratch_bytes = (_nbytes((chunk + SUBLANES, ch), F32) + _nbytes((chunk, ch), F32)
                     + _nbytes((SSD_GROUPS, SSD_STATE, gw), F32))
    temp_bytes = 4 * _nbytes((chunk, d_ssd), F32) + 16 * _nbytes((chunk, max(chunk, LANES)), F32)
    kern = functools.partial(_ssd_kernel, chunk=chunk, d_ssd=d_ssd, col_slab=col_slab)
    y, state = pl.pallas_call(
        kern,
        grid=(b, n_chunks),
        in_specs=[
            pl.BlockSpec((None, chunk, ch), lambda bi, c: (bi, c, 0)),
            pl.BlockSpec((None, chunk, LANES), lambda bi, c: (bi, c, 0)),
            pl.BlockSpec((None, CONV_WIDTH - 1, ch), lambda bi, c: (bi, 0, 0)),
            pl.BlockSpec((None, SSD_GROUPS, gw, SSD_STATE), lambda bi, c: (bi, 0, 0, 0)),
            full((CONV_WIDTH, ch)), full((1, ch)), full((1, LANES)), full((1, LANES)), full((1, d_ssd)),
            full((LANES, d_ssd)), full((chunk, chunk)),
        ],
        out_specs=[
            pl.BlockSpec((None, chunk, d_ssd), lambda bi, c: (bi, c, 0)),
            pl.BlockSpec((None, SSD_GROUPS, gw, SSD_STATE), lambda bi, c: (bi, 0, 0, 0)),
        ],
        out_shape=[jax.ShapeDtypeStruct((b, t, d_ssd), F32),
                   jax.ShapeDtypeStruct((b, SSD_GROUPS, gw, SSD_STATE), F32)],
        scratch_shapes=[pltpu.VMEM((chunk + SUBLANES, ch), F32), pltpu.VMEM((chunk, ch), F32),
                        pltpu.VMEM((SSD_GROUPS, SSD_STATE, gw), F32)],
        compiler_params=pltpu.CompilerParams(
            dimension_semantics=("arbitrary", "arbitrary"),
            vmem_limit_bytes=_vmem_limit(block_bytes, scratch_bytes, temp_bytes)),
        name="conv_ssd",
    )(xbc, dt_raw, conv_past, state_g, conv_w, conv_b.reshape(1, ch), dtb, alog, dskip, expand, tri)
    return y, state.reshape(b, heads, SSD_HEAD_DIM, SSD_STATE)


def _gate_norm_kernel(o_ref, g_ref, y_ref, z_ref, nsb_ref, nssd_ref, a_ref, *, d_sb, row_chunk):
    tm = o_ref.shape[0]

    def body(r, carry):
        rows = pl.ds(pl.multiple_of(r * row_chunk, row_chunk), row_chunk)
        u = o_ref[rows, :] * _silu(g_ref[rows, :])
        ms = jnp.mean(u * u, axis=-1, keepdims=True)
        a_ref[rows, 0:d_sb] = (u * lax.rsqrt(ms + EPS) * nsb_ref[...]).astype(BF16)
        w = y_ref[rows, :] * _silu(z_ref[rows, :])
        ms2 = jnp.mean(w * w, axis=-1, keepdims=True)
        a_ref[rows, d_sb:] = (w * lax.rsqrt(ms2 + EPS) * nssd_ref[...]).astype(BF16)
        return carry

    lax.fori_loop(0, tm // row_chunk, body, 0)


def _gate_norm(o, g, y, z, sb_norm, ssd_norm, *, tm):
    m, d_sb = o.shape
    d_ssd = y.shape[1]
    assert m % tm == 0
    block_bytes = 2 * _nbytes((tm, d_sb), F32) + 2 * _nbytes((tm, d_ssd), F32) + _nbytes((tm, d_sb + d_ssd), BF16)
    kern = functools.partial(_gate_norm_kernel, d_sb=d_sb, row_chunk=16)
    return pl.pallas_call(
        kern,
        grid=(m // tm,),
        in_specs=[
            pl.BlockSpec((tm, d_sb), lambda i: (i, 0)), pl.BlockSpec((tm, d_sb), lambda i: (i, 0)),
            pl.BlockSpec((tm, d_ssd), lambda i: (i, 0)), pl.BlockSpec((tm, d_ssd), lambda i: (i, 0)),
            pl.BlockSpec((1, d_sb), lambda i: (0, 0)), pl.BlockSpec((1, d_ssd), lambda i: (0, 0)),
        ],
        out_specs=pl.BlockSpec((tm, d_sb + d_ssd), lambda i: (i, 0)),
        out_shape=jax.ShapeDtypeStruct((m, d_sb + d_ssd), BF16),
        compiler_params=pltpu.CompilerParams(
            dimension_semantics=("arbitrary",),
            vmem_limit_bytes=_vmem_limit(block_bytes, 0, 8 * _nbytes((16, d_sb + d_ssd), F32))),
        name="gate_norm",
    )(o, g, y, z, sb_norm.reshape(1, d_sb), ssd_norm.reshape(1, d_ssd))


def _out_proj_kernel(a_ref, w_ref, x_ref, nw_ref, x1_ref, x1b_ref, m_ref, ssq_ref, *, n_col, d_model):
    j = pl.program_id(1)

    @pl.when(j == 0)
    def _():
        ssq_ref[...] = jnp.zeros_like(ssq_ref)

    @pl.when(j < n_col)
    def _matmul():
        mt = _dot(a_ref[...], w_ref[...])
        m_ref[j] = mt
        ssq_ref[...] += jnp.sum(mt * mt, axis=-1, keepdims=True)

    @pl.when(j >= n_col)
    def _finish():
        inv = lax.rsqrt(ssq_ref[...] * (1.0 / d_model) + EPS)
        x1 = x_ref[...] + m_ref[j - n_col] * inv * nw_ref[...]
        x1_ref[...] = x1
        x1b_ref[...] = x1.astype(BF16)


def _out_proj(a, w, x, norm_w, *, tm, tn):
    m, d_mix = a.shape
    d = w.shape[1]
    assert m % tm == 0 and d % tn == 0
    n_col = d // tn
    w_map = lambda i, j: (0, jnp.minimum(j, n_col - 1))
    t_map = lambda i, j: (i, jnp.maximum(j - n_col, 0))
    block_bytes = (_nbytes((tm, d_mix), BF16) + _nbytes((d_mix, tn), BF16) + 2 * _nbytes((tm, tn), F32)
                   + _nbytes((tm, tn), BF16) + _nbytes((SUBLANES, tn), F32))
    scratch_bytes = _nbytes((tm, d), F32) + _nbytes((tm, LANES), F32)
    kern = functools.partial(_out_proj_kernel, n_col=n_col, d_model=d)
    return pl.pallas_call(
        kern,
        grid=(m // tm, 2 * n_col),
        in_specs=[
            pl.BlockSpec((tm, d_mix), lambda i, j: (i, 0)),
            pl.BlockSpec((d_mix, tn), w_map),
            pl.BlockSpec((tm, tn), t_map),
            pl.BlockSpec((1, tn), lambda i, j: (0, jnp.maximum(j - n_col, 0))),
        ],
        out_specs=[pl.BlockSpec((tm, tn), t_map), pl.BlockSpec((tm, tn), t_map)],
        out_shape=[jax.ShapeDtypeStruct((m, d), F32), jax.ShapeDtypeStruct((m, d), BF16)],
        scratch_shapes=[pltpu.VMEM((n_col, tm, tn), F32), pltpu.VMEM((tm, 1), F32)],
        compiler_params=pltpu.CompilerParams(
            dimension_semantics=("arbitrary", "arbitrary"),
            vmem_limit_bytes=_vmem_limit(block_bytes, scratch_bytes, 2 * _nbytes((tm, tn), F32))),
        name="out_proj",
    )(a, w, x, norm_w.reshape(1, d))


def _ple_kernel(x1b_ref, wg_ref, p_ref, wp_ref, x1_ref, nw_ref, y_ref, ge_ref, ssq_ref, *, n_col, d_model):
    j = pl.program_id(1)

    @pl.when(j == 0)
    def _():
        ssq_ref[...] = jnp.zeros_like(ssq_ref)

    @pl.when(j < n_col)
    def _matmul():
        gate = _sigmoid(_dot(x1b_ref[...], wg_ref[...]))
        emb = _dot(p_ref[...].astype(BF16), wp_ref[...])
        ge = gate * emb
        ge_ref[j] = ge
        ssq_ref[...] += jnp.sum(ge * ge, axis=-1, keepdims=True)

    @pl.when(j >= n_col)
    def _finish():
        inv = lax.rsqrt(ssq_ref[...] * (1.0 / d_model) + EPS)
        y_ref[...] = x1_ref[...] + ge_ref[j - n_col] * inv * nw_ref[...]


def _ple(x1b, w_gate, p, w_proj, x1, norm_w, *, tm, tn):
    m, d = x1.shape
    dp = p.shape[1]
    assert m % tm == 0 and d % tn == 0
    n_col = d // tn
    w_map = lambda i, j: (0, jnp.minimum(j, n_col - 1))
    t_map = lambda i, j: (i, jnp.maximum(j - n_col, 0))
    block_bytes = (_nbytes((tm, d), BF16) + _nbytes((d, tn), BF16) + _nbytes((tm, dp), F32) + _nbytes((dp, tn), BF16)
                   + 2 * _nbytes((tm, tn), F32) + _nbytes((SUBLANES, tn), F32))
    scratch_bytes = _nbytes((tm, d), F32) + _nbytes((tm, LANES), F32)
    kern = functools.partial(_ple_kernel, n_col=n_col, d_model=d)
    return pl.pallas_call(
        kern,
        grid=(m // tm, 2 * n_col),
        in_specs=[
            pl.BlockSpec((tm, d), lambda i, j: (i, 0)),
            pl.BlockSpec((d, tn), w_map),
            pl.BlockSpec((tm, dp), lambda i, j: (i, 0)),
            pl.BlockSpec((dp, tn), w_map),
            pl.BlockSpec((tm, tn), t_map),
            pl.BlockSpec((1, tn), lambda i, j: (0, jnp.maximum(j - n_col, 0))),
        ],
        out_specs=pl.BlockSpec((tm, tn), t_map),
        out_shape=jax.ShapeDtypeStruct((m, d), F32),
        scratch_shapes=[pltpu.VMEM((n_col, tm, tn), F32), pltpu.VMEM((tm, 1), F32)],
        compiler_params=pltpu.CompilerParams(
            dimension_semantics=("arbitrary", "arbitrary"),
            vmem_limit_bytes=_vmem_limit(block_bytes, scratch_bytes, 3 * _nbytes((tm, tn), F32))),
        name="ple",
    )(x1b, w_gate, p, w_proj, x1, norm_w.reshape(1, d))


def _row_tile(m, target):
    return target if m % target == 0 else m


def _trunk_layer(x, p, k_past, v_past, conv_past, ssd_past, wts, *, d_sb, d_ssd, conv_ch):
    (norm_pre, norm_post, w_main, w_dt, conv_w, conv_b, dt_bias, a_log, d_skip, sb_norm, ssd_norm,
     w_out, w_gate, w_proj, ple_norm) = wts
    b, t, d = x.shape
    m = b * t
    xf = x.reshape(m, d)
    tm = _row_tile(m, 512)
    q, k, v, g, xbc, z, dt_raw = _in_proj(xf, norm_pre.reshape(1, d), w_main, w_dt,
                                          d_sb=d_sb, conv_ch=conv_ch, d_ssd=d_ssd, tm=tm)
    tq = 256 if t % 256 == 0 else t
    kp = None if k_past is None else k_past.reshape(b, k_past.shape[1], d_sb)
    vp = None if v_past is None else v_past.reshape(b, v_past.shape[1], d_sb)
    o = _attention(q.reshape(b, t, d_sb), k.reshape(b, t, d_sb), v.reshape(b, t, d_sb), kp, vp, tq=tq, tp=256)
    chunk = 128 if t % 128 == 0 else t
    y, ssd_new = _ssd(xbc.reshape(b, t, conv_ch), dt_raw.reshape(b, t, LANES), conv_past, ssd_past,
                      conv_w, conv_b, dt_bias, a_log, d_skip, d_ssd=d_ssd, chunk=chunk)
    a = _gate_norm(o.reshape(m, d_sb), g, y.reshape(m, d_ssd), z, sb_norm, ssd_norm, tm=_row_tile(m, 256))
    x1, x1b = _out_proj(a, w_out, xf, norm_post, tm=tm, tn=512 if d % 512 == 0 else d)
    yo = _ple(x1b, w_gate, p.reshape(m, p.shape[-1]), w_proj, x1, ple_norm, tm=tm, tn=512 if d % 512 == 0 else d)
    heads = d_sb // SB_HEAD_DIM
    conv_new = xbc.reshape(b, t, conv_ch)[:, t - (CONV_WIDTH - 1):, :]
    return (yo.reshape(b, t, d), k.reshape(b, t, heads, SB_HEAD_DIM), v.reshape(b, t, heads, SB_HEAD_DIM),
            conv_new, ssd_new)


def kernel(x_prompt, x_sample, cache_k, cache_v, state_conv, state_ssd, p_prompt, p_sample, norm_pre, norm_post,
           w_in, conv_w, conv_b, dt_bias, a_log, d_skip, sb_norm, ssd_norm, w_out, w_ple_gate, w_ple_proj, ple_norm):
    depth = norm_pre.shape[0]
    ssd_heads = dt_bias.shape[1]
    d_ssd = ssd_heads * SSD_HEAD_DIM
    d_sb = sb_norm.shape[1]
    conv_ch = conv_w.shape[2]
    n_main = 4 * d_sb + conv_ch + d_ssd
    bp = x_prompt.shape[0]
    yp, ys = x_prompt, x_sample
    outs = [[] for _ in range(8)]
    for i in range(depth):
        w_main = w_in[i][:, :n_main].astype(BF16)
        w_dt = jnp.pad(w_in[i][:, n_main:], ((0, 0), (0, LANES - ssd_heads))).astype(BF16)
        wts = (norm_pre[i], norm_post[i], w_main, w_dt, conv_w[i], conv_b[i], dt_bias[i], a_log[i], d_skip[i],
               sb_norm[i], ssd_norm[i], w_out[i].astype(BF16), w_ple_gate[i].astype(BF16),
               w_ple_proj[i].astype(BF16), ple_norm[i])
        dims = dict(d_sb=d_sb, d_ssd=d_ssd, conv_ch=conv_ch)
        zero_conv = jnp.zeros((bp, CONV_WIDTH - 1, conv_ch), F32)
        zero_ssd = jnp.zeros((bp, ssd_heads, SSD_HEAD_DIM, SSD_STATE), F32)
        yp, kp, vp, cp, sp = _trunk_layer(yp, p_prompt[i], None, None, zero_conv, zero_ssd, wts, **dims)
        ys, ks, vs, cs, ss = _trunk_layer(ys, p_sample[i], cache_k[i], cache_v[i], state_conv[i], state_ssd[i],
                                          wts, **dims)
        for lst, val in zip(outs, (kp, vp, cp, sp, ks, vs, cs, ss)):
            lst.append(val)
    stacked = [jnp.stack(lst) for lst in outs]
    return (yp, ys, *stacked)
```

```python
import functools
import math

import jax
import jax.numpy as jnp
from jax import lax
from jax.experimental import pallas as pl
from jax.experimental.pallas import tpu as pltpu

F32 = jnp.float32
BF16 = jnp.bfloat16

EPS = 1e-6
SB_HEAD_DIM = 128
SSD_HEAD_DIM = 64
SSD_STATE = 128
SSD_GROUPS = 8
CONV_WIDTH = 4
LANES = 128
SUBLANES = 8
V7X_VMEM_BYTES = 64 * 2**20


def _vmem_limit(block_bytes, scratch_bytes, temp_bytes):
    need = 2 * block_bytes + scratch_bytes + temp_bytes
    return int(min(need, V7X_VMEM_BYTES - 2 * 2**20))


def _nbytes(shape, dtype):
    return math.prod(shape) * jnp.dtype(dtype).itemsize


def _sigmoid(x):
    return 1.0 / (1.0 + jnp.exp(-x))


def _silu(x):
    return x * _sigmoid(x)


def _split_bf16(x, parts):
    out = []
    r = x
    for p in range(parts):
        h = r.astype(BF16)
        out.append(h)
        if p + 1 < parts:
            r = r - h.astype(F32)
    return out


def _dot(a, b):
    return jnp.dot(a, b, preferred_element_type=F32)


def _dot_split(x, m, parts):
    acc = None
    for h in _split_bf16(x, parts):
        t = _dot(h, m)
        acc = t if acc is None else acc + t
    return acc


def _prenorm_kernel(x_ref, nw_ref, h_ref, *, row_chunk):
    tm = x_ref.shape[0]

    def body(r, carry):
        rows = pl.ds(pl.multiple_of(r * row_chunk, row_chunk), row_chunk)
        xf = x_ref[rows, :]
        ms = jnp.mean(xf * xf, axis=-1, keepdims=True)
        h_ref[rows, :] = (xf * lax.rsqrt(ms + EPS) * nw_ref[...]).astype(BF16)
        return carry

    lax.fori_loop(0, tm // row_chunk, body, 0, unroll=2)


def _prenorm(x, norm_w, *, tm):
    m, d = x.shape
    assert m % tm == 0
    block_bytes = _nbytes((tm, d), F32) + _nbytes((tm, d), BF16)
    return pl.pallas_call(
        functools.partial(_prenorm_kernel, row_chunk=32),
        grid=(m // tm,),
        in_specs=[pl.BlockSpec((tm, d), lambda i: (i, 0)), pl.BlockSpec((1, d), lambda i: (0, 0))],
        out_specs=pl.BlockSpec((tm, d), lambda i: (i, 0)),
        out_shape=jax.ShapeDtypeStruct((m, d), BF16),
        compiler_params=pltpu.CompilerParams(
            dimension_semantics=("arbitrary",),
            vmem_limit_bytes=_vmem_limit(block_bytes, 0, 8 * _nbytes((32, d), F32))),
        name="prenorm",
    )(x, norm_w)


def _in_proj_kernel(h_ref, w_ref, wdt_ref, q_ref, k_ref, v_ref, g_ref, xbc_ref, z_ref, dt_ref, *, bounds, q_scale):
    j = pl.program_id(1)

    def segment(lo, hi, out_ref, scale):
        @pl.when((j >= lo) & (j < hi))
        def _():
            acc = _dot(h_ref[...], w_ref[...])
            if scale is not None:
                acc = acc * scale
            out_ref[...] = acc.astype(out_ref.dtype)

    outs = (q_ref, k_ref, v_ref, g_ref, xbc_ref, z_ref)
    for s, out_ref in enumerate(outs):
        segment(bounds[s], bounds[s + 1], out_ref, q_scale if s == 0 else None)

    @pl.when(j == bounds[-1])
    def _dt():
        dt_ref[...] = _dot(h_ref[...], wdt_ref[...])


def _in_proj(h, w_main, w_dt, *, d_sb, conv_ch, d_ssd, tm):
    m, d = h.shape
    tn = math.gcd(math.gcd(d_sb, 512), math.gcd(conv_ch, d_ssd))
    widths = (d_sb, d_sb, d_sb, d_sb, conv_ch, d_ssd)
    bounds = [0]
    for w in widths:
        bounds.append(bounds[-1] + w // tn)
    bounds = tuple(bounds)
    n_tiles = bounds[-1]
    assert w_main.shape == (d, n_tiles * tn) and w_dt.shape == (d, LANES) and m % tm == 0

    def out_map(lo, hi):
        return lambda i, j: (i, jnp.clip(j - lo, 0, hi - lo - 1))

    out_shapes = [jax.ShapeDtypeStruct((m, d_sb), BF16)]
    out_shapes += [jax.ShapeDtypeStruct((m, wd), F32) for wd in widths[1:]]
    out_shapes.append(jax.ShapeDtypeStruct((m, LANES), F32))
    out_specs = [pl.BlockSpec((tm, tn), out_map(bounds[s], bounds[s + 1])) for s in range(len(widths))]
    out_specs.append(pl.BlockSpec((tm, LANES), lambda i, j: (i, 0)))

    block_bytes = (_nbytes((tm, d), BF16) + _nbytes((d, tn), BF16) + _nbytes((d, LANES), BF16)
                   + _nbytes((tm, tn), BF16) + 5 * _nbytes((tm, tn), F32) + _nbytes((tm, LANES), F32))
    temp_bytes = 2 * _nbytes((tm, tn), F32)
    kern = functools.partial(_in_proj_kernel, bounds=bounds, q_scale=SB_HEAD_DIM ** -0.5)
    return pl.pallas_call(
        kern,
        grid=(m // tm, n_tiles + 1),
        in_specs=[
            pl.BlockSpec((tm, d), lambda i, j: (i, 0)),
            pl.BlockSpec((d, tn), lambda i, j: (0, jnp.minimum(j, n_tiles - 1))),
            pl.BlockSpec((d, LANES), lambda i, j: (0, 0)),
        ],
        out_specs=out_specs,
        out_shape=out_shapes,
        compiler_params=pltpu.CompilerParams(
            dimension_semantics=("arbitrary", "arbitrary"),
            vmem_limit_bytes=_vmem_limit(block_bytes, 0, temp_bytes)),
        name="in_proj",
    )(h, w_main, w_dt)


def _attn_kernel(*refs, tq, tp, n_past, nh):
    if n_past:
        q_ref, k_ref, v_ref, kp_ref, vp_ref, u_ref, o_ref, acc_ref, carry_ref = refs
    else:
        q_ref, k_ref, v_ref, u_ref, o_ref, acc_ref, carry_ref = refs
    d = SB_HEAD_DIM
    qi = pl.program_id(2)
    acc_ref[...] = jnp.zeros_like(acc_ref)
    carry_ref[...] = jnp.zeros_like(carry_ref)

    def block(ks_ref, vs_ref, rows, u, mask):
        heads = [slice(n * d, (n + 1) * d) for n in range(nh)]
        qs = [q_ref[:, hs] for hs in heads]
        kbs = [ks_ref[rows, hs].astype(BF16) for hs in heads]
        vbs = [vs_ref[rows, hs].astype(BF16) for hs in heads]
        carries = [carry_ref[n] for n in range(nh)]
        accs = [acc_ref[:, hs] for hs in heads]
        new_acc, new_carry = [], []
        for n in range(nh):
            z = lax.dot_general(qs[n], kbs[n], (((1,), (1,)), ((), ())), preferred_element_type=F32)
            sp = jnp.log(1.0 + jnp.exp(-jnp.abs(z)))
            log_stop = jnp.minimum(z, 0.0) - sp
            log_stay = log_stop - z
            if mask is not None:
                log_stay = jnp.where(mask, log_stay, 0.0)
            later = _dot_split(log_stay, u, 2)
            att = jnp.exp(log_stop + later + carries[n])
            if mask is not None:
                att = jnp.where(mask, att, 0.0)
            new_acc.append(accs[n] + _dot(att.astype(BF16), vbs[n]))
            new_carry.append(carries[n] + jnp.sum(log_stay, axis=-1, keepdims=True))
        for n in range(nh):
            acc_ref[:, heads[n]] = new_acc[n]
            carry_ref[n] = new_carry[n]

    row = lax.broadcasted_iota(jnp.int32, (tq, tq), 0)
    col = lax.broadcasted_iota(jnp.int32, (tq, tq), 1)
    u_new = u_ref[:tq, :tq]
    block(k_ref, v_ref, pl.ds(pl.multiple_of(qi * tq, tq), tq), u_new, col < row)

    def new_body(i, c):
        block(k_ref, v_ref, pl.ds(pl.multiple_of((qi - 1 - i) * tq, tq), tq), u_new, None)
        return c

    lax.fori_loop(0, qi, new_body, 0)

    if n_past:
        u_past = u_ref[:tp, :tp]

        def past_body(i, c):
            block(kp_ref, vp_ref, pl.ds(pl.multiple_of((n_past - 1 - i) * tp, tp), tp), u_past, None)
            return c

        lax.fori_loop(0, n_past, past_body, 0)

    o_ref[...] = acc_ref[...]


def _attention(q, k, v, k_past, v_past, *, tq, tp, nh):
    b, t, hd = k.shape
    d = SB_HEAD_DIM
    heads = hd // d
    p = 0 if k_past is None else k_past.shape[1]
    assert t % tq == 0 and p % tp == 0 and heads % nh == 0
    n_past = p // tp
    tu = max(tq, tp) if n_past else tq
    ridx = lax.broadcasted_iota(jnp.int32, (tu, tu), 0)
    cidx = lax.broadcasted_iota(jnp.int32, (tu, tu), 1)
    u = (ridx > cidx).astype(BF16)
    w = nh * d

    q_spec = pl.BlockSpec((None, tq, w), lambda bi, h, qi: (bi, qi, h))
    kv_spec = pl.BlockSpec((None, t, w), lambda bi, h, qi: (bi, 0, h))
    in_specs = [q_spec, kv_spec, kv_spec]
    args = [q, k, v]
    block_bytes = _nbytes((tq, w), BF16) + 2 * _nbytes((t, w), F32) + _nbytes((tu, tu), BF16) + _nbytes((tq, w), F32)
    if n_past:
        past_spec = pl.BlockSpec((None, p, w), lambda bi, h, qi: (bi, 0, h))
        in_specs += [past_spec, past_spec]
        args += [k_past, v_past]
        block_bytes += 2 * _nbytes((p, w), F32)
    in_specs.append(pl.BlockSpec((tu, tu), lambda bi, h, qi: (0, 0)))
    args.append(u)
    scratch_bytes = _nbytes((tq, w), F32) + nh * _nbytes((tq, LANES), F32)
    temp_bytes = 8 * nh * _nbytes((tq, tu), F32)
    kern = functools.partial(_attn_kernel, tq=tq, tp=tp, n_past=n_past, nh=nh)
    return pl.pallas_call(
        kern,
        grid=(b, heads // nh, t // tq),
        in_specs=in_specs,
        out_specs=pl.BlockSpec((None, tq, w), lambda bi, h, qi: (bi, qi, h)),
        out_shape=jax.ShapeDtypeStruct((b, t, hd), F32),
        scratch_shapes=[pltpu.VMEM((tq, w), F32), pltpu.VMEM((nh, tq, 1), F32)],
        compiler_params=pltpu.CompilerParams(
            dimension_semantics=("arbitrary", "arbitrary", "arbitrary"),
            vmem_limit_bytes=_vmem_limit(block_bytes, scratch_bytes, temp_bytes)),
        name="stick_breaking",
    )(*args)


def _ssd_kernel(xbc_ref, dt_ref, convp_ref, statep_ref, cw_ref, cb_ref, dtb_ref, alog_ref, dskip_ref, e_ref, tri_ref,
                y_ref, state_ref, ext_ref, act_ref, ht_ref, *, chunk, d_ssd, col_slab):
    c = pl.program_id(1)
    n_chunks = pl.num_programs(1)
    ch = xbc_ref.shape[-1]
    gn = SSD_GROUPS * SSD_STATE
    hpg = d_ssd // SSD_HEAD_DIM // SSD_GROUPS
    gw = hpg * SSD_HEAD_DIM
    pad = SUBLANES

    @pl.when(c == 0)
    def _init():
        ext_ref[0:pad, :] = jnp.zeros((pad, ch), F32)
        ext_ref[pad - (CONV_WIDTH - 1):pad, :] = convp_ref[...]
        for g in range(SSD_GROUPS):
            ht_ref[g] = statep_ref[g].T

    @pl.when(c > 0)
    def _shift():
        ext_ref[0:pad, :] = ext_ref[chunk:chunk + pad, :]

    ext_ref[pad:pad + chunk, :] = xbc_ref[...]

    for s in range(ch // col_slab):
        cols = slice(s * col_slab, (s + 1) * col_slab)
        acc = cb_ref[:, cols]
        for jj in range(CONV_WIDTH):
            lo = pad - (CONV_WIDTH - 1) + jj
            acc = acc + cw_ref[jj:jj + 1, cols] * ext_ref[lo:lo + chunk, cols]
        act_ref[:, cols] = _silu(acc)

    dt_in = dt_ref[...] + dtb_ref[...]
    dt = jnp.maximum(dt_in, 0.0) + jnp.log(1.0 + jnp.exp(-jnp.abs(dt_in)))
    da = dt * (-jnp.exp(alog_ref[...]))
    tri = tri_ref[...]
    acum = _dot_tri(tri, da)
    last = acum[chunk - 1:chunk, :]
    acum_t = acum.T
    dt_t = dt.T
    row = lax.broadcasted_iota(jnp.int32, (chunk, chunk), 0)
    col = lax.broadcasted_iota(jnp.int32, (chunk, chunk), 1)
    causal = col <= row

    e = e_ref[...]
    de_e = _dot_split(jnp.exp(last - acum) * dt, e, 2)
    ea_e = _dot_split(jnp.exp(acum), e, 2)
    bd_e = _dot_split(jnp.broadcast_to(jnp.exp(last), (SUBLANES, LANES)), e, 2)[0:1, :]

    lane = lax.broadcasted_iota(jnp.int32, (chunk, gw), 1)
    for g in range(SSD_GROUPS):
        gs = slice(g * gw, (g + 1) * gw)
        xg = act_ref[:, gs]
        bg = act_ref[:, d_ssd + g * SSD_STATE:d_ssd + (g + 1) * SSD_STATE].astype(BF16)
        cg = act_ref[:, d_ssd + gn + g * SSD_STATE:d_ssd + gn + (g + 1) * SSD_STATE].astype(BF16)
        cb = lax.dot_general(cg, bg, (((1,), (1,)), ((), ())), preferred_element_type=F32)
        xgb = xg.astype(BF16)
        y_intra = None
        for hh in range(hpg):
            h = g * hpg + hh
            diff = acum[:, h:h + 1] - acum_t[h:h + 1, :]
            decay = jnp.exp(jnp.where(causal, diff, -jnp.inf))
            w = (cb * decay * dt_t[h:h + 1, :]).astype(BF16)
            in_head = (lane >= hh * SSD_HEAD_DIM) & (lane < (hh + 1) * SSD_HEAD_DIM)
            t = _dot(w, jnp.where(in_head, xgb, jnp.zeros_like(xgb)))
            y_intra = t if y_intra is None else y_intra + t
        h_prev = ht_ref[g]
        y_inter = _dot(cg, h_prev.astype(BF16)) * ea_e[:, gs]
        xw = (xg * de_e[:, gs]).astype(BF16)
        s_t = lax.dot_general(bg, xw, (((0,), (0,)), ((), ())), preferred_element_type=F32)
        ht_ref[g] = h_prev * bd_e[:, gs] + s_t
        y_ref[:, gs] = y_intra + y_inter + dskip_ref[:, gs] * xg

    @pl.when(c == n_chunks - 1)
    def _final():
        for g in range(SSD_GROUPS):
            state_ref[g] = ht_ref[g].T


def _dot_tri(tri, x):
    acc = None
    for h in _split_bf16(x, 3):
        t = _dot(tri, h)
        acc = t if acc is None else acc + t
    return acc


def _ssd(xbc, dt_raw, conv_past, state_past, conv_w, conv_b, dt_bias, a_log, d_skip, *, d_ssd, chunk):
    b, t, ch = xbc.shape
    heads = d_ssd // SSD_HEAD_DIM
    hpg = heads // SSD_GROUPS
    gw = hpg * SSD_HEAD_DIM
    assert t % chunk == 0 and heads <= LANES
    n_chunks = t // chunk
    pad_h = LANES - heads
    dtb = jnp.pad(dt_bias.astype(F32), (0, pad_h)).reshape(1, LANES)
    alog = jnp.pad(a_log.astype(F32), (0, pad_h)).reshape(1, LANES)
    dskip = jnp.repeat(d_skip.astype(F32), SSD_HEAD_DIM).reshape(1, d_ssd)
    hidx = lax.broadcasted_iota(jnp.int32, (LANES, d_ssd), 0)
    lidx = lax.broadcasted_iota(jnp.int32, (LANES, d_ssd), 1)
    expand = (lidx // SSD_HEAD_DIM == hidx).astype(BF16)
    ridx = lax.broadcasted_iota(jnp.int32, (chunk, chunk), 0)
    cidx = lax.broadcasted_iota(jnp.int32, (chunk, chunk), 1)
    tri = (cidx <= ridx).astype(BF16)
    state_g = state_past.reshape(b, SSD_GROUPS, gw, SSD_STATE)

    full = lambda shape: pl.BlockSpec(shape, lambda bi, c: (0,) * len(shape))
    col_slab = 512
    block_bytes = (_nbytes((chunk, ch), F32) + _nbytes((chunk, LANES), F32) + _nbytes((SUBLANES, ch), F32)
                   + 2 * _nbytes((SSD_GROUPS, gw, SSD_STATE), F32) + _nbytes((CONV_WIDTH + 1, ch), F32) * 2
                   + _nbytes((LANES, d_ssd), BF16) + _nbytes((chunk, chunk), BF16) + _nbytes((chunk, d_ssd), F32))
    scratch_bytes = (_nbytes((chunk + SUBLANES, ch), F32) + _nbytes((chunk, ch), F32)
                     + _nbytes((SSD_GROUPS, SSD_STATE, gw), F32))
    temp_bytes = 4 * _nbytes((chunk, d_ssd), F32) + 16 * _nbytes((chunk, max(chunk, LANES)), F32)
    kern = functools.partial(_ssd_kernel, chunk=chunk, d_ssd=d_ssd, col_slab=col_slab)
    y, state = pl.pallas_call(
        kern,
        grid=(b, n_chunks),
        in_specs=[
            pl.BlockSpec((None, chunk, ch), lambda bi, c: (bi, c, 0)),
            pl.BlockSpec((None, chunk, LANES), lambda bi, c: (bi, c, 0)),
            pl.BlockSpec((None, CONV_WIDTH - 1, ch), lambda bi, c: (bi, 0, 0)),
            pl.BlockSpec((None, SSD_GROUPS, gw, SSD_STATE), lambda bi, c: (bi, 0, 0, 0)),
            full((CONV_WIDTH, ch)), full((1, ch)), full((1, LANES)), full((1, LANES)), full((1, d_ssd)),
            full((LANES, d_ssd)), full((chunk, chunk)),
        ],
        out_specs=[
            pl.BlockSpec((None, chunk, d_ssd), lambda bi, c: (bi, c, 0)),
            pl.BlockSpec((None, SSD_GROUPS, gw, SSD_STATE), lambda bi, c: (bi, 0, 0, 0)),
        ],
        out_shape=[jax.ShapeDtypeStruct((b, t, d_ssd), F32),
                   jax.ShapeDtypeStruct((b, SSD_GROUPS, gw, SSD_STATE), F32)],
        scratch_shapes=[pltpu.VMEM((chunk + SUBLANES, ch), F32), pltpu.VMEM((chunk, ch), F32),
                        pltpu.VMEM((SSD_GROUPS, SSD_STATE, gw), F32)],
        compiler_params=pltpu.CompilerParams(
            dimension_semantics=("arbitrary", "arbitrary"),
            vmem_limit_bytes=_vmem_limit(block_bytes, scratch_bytes, temp_bytes)),
        name="conv_ssd",
    )(xbc, dt_raw, conv_past, state_g, conv_w, conv_b.reshape(1, ch), dtb, alog, dskip, expand, tri)
    return y, state.reshape(b, heads, SSD_HEAD_DIM, SSD_STATE)


def _gate_norm_kernel(o_ref, g_ref, y_ref, z_ref, nsb_ref, nssd_ref, a_ref, *, d_sb, row_chunk):
    tm = o_ref.shape[0]

    def body(r, carry):
        rows = pl.ds(pl.multiple_of(r * row_chunk, row_chunk), row_chunk)
        u = o_ref[rows, :] * _silu(g_ref[rows, :])
        ms = jnp.mean(u * u, axis=-1, keepdims=True)
        a_ref[rows, 0:d_sb] = (u * lax.rsqrt(ms + EPS) * nsb_ref[...]).astype(BF16)
        w = y_ref[rows, :] * _silu(z_ref[rows, :])
        ms2 = jnp.mean(w * w, axis=-1, keepdims=True)
        a_ref[rows, d_sb:] = (w * lax.rsqrt(ms2 + EPS) * nssd_ref[...]).astype(BF16)
        return carry

    lax.fori_loop(0, tm // row_chunk, body, 0)


def _gate_norm(o, g, y, z, sb_norm, ssd_norm, *, tm):
    m, d_sb = o.shape
    d_ssd = y.shape[1]
    assert m % tm == 0
    block_bytes = 2 * _nbytes((tm, d_sb), F32) + 2 * _nbytes((tm, d_ssd), F32) + _nbytes((tm, d_sb + d_ssd), BF16)
    kern = functools.partial(_gate_norm_kernel, d_sb=d_sb, row_chunk=16)
    return pl.pallas_call(
        kern,
        grid=(m // tm,),
        in_specs=[
            pl.BlockSpec((tm, d_sb), lambda i: (i, 0)), pl.BlockSpec((tm, d_sb), lambda i: (i, 0)),
            pl.BlockSpec((tm, d_ssd), lambda i: (i, 0)), pl.BlockSpec((tm, d_ssd), lambda i: (i, 0)),
            pl.BlockSpec((1, d_sb), lambda i: (0, 0)), pl.BlockSpec((1, d_ssd), lambda i: (0, 0)),
        ],
        out_specs=pl.BlockSpec((tm, d_sb + d_ssd), lambda i: (i, 0)),
        out_shape=jax.ShapeDtypeStruct((m, d_sb + d_ssd), BF16),
        compiler_params=pltpu.CompilerParams(
            dimension_semantics=("arbitrary",),
            vmem_limit_bytes=_vmem_limit(block_bytes, 0, 8 * _nbytes((16, d_sb + d_ssd), F32))),
        name="gate_norm",
    )(o, g, y, z, sb_norm.reshape(1, d_sb), ssd_norm.reshape(1, d_ssd))


def _lagged_maps(n_rows, n_col):
    cur = lambda i, j: (jnp.minimum(i, n_rows - 1), 0)
    wcol = lambda i, j: (0, jnp.where(i < n_rows, j, n_col - 1))
    prev = lambda i, j: (jnp.maximum(i - 1, 0), jnp.where(i > 0, j, 0))
    ncol = lambda i, j: (0, jnp.where(i > 0, j, 0))
    return cur, wcol, prev, ncol


def _out_proj_kernel(a_ref, w_ref, x_ref, nw_ref, x1_ref, x1b_ref, m_ref, ssq_ref, inv_ref, *, n_rows, d_model):
    i = pl.program_id(0)
    j = pl.program_id(1)

    @pl.when(j == 0)
    def _():
        @pl.when(i > 0)
        def _():
            inv_ref[...] = lax.rsqrt(ssq_ref[...] * (1.0 / d_model) + EPS)

        ssq_ref[...] = jnp.zeros_like(ssq_ref)

    @pl.when(i > 0)
    def _finish():
        x1 = x_ref[...] + m_ref[j] * inv_ref[...] * nw_ref[...]
        x1_ref[...] = x1
        x1b_ref[...] = x1.astype(BF16)

    @pl.when(i < n_rows)
    def _matmul():
        mt = _dot(a_ref[...], w_ref[...])
        m_ref[j] = mt
        ssq_ref[...] += jnp.sum(mt * mt, axis=-1, keepdims=True)


def _out_proj(a, w, x, norm_w, *, tm, tn):
    m, d_mix = a.shape
    d = w.shape[1]
    assert m % tm == 0 and d % tn == 0
    n_col = d // tn
    n_rows = m // tm
    cur, wcol, prev, ncol = _lagged_maps(n_rows, n_col)
    block_bytes = (_nbytes((tm, d_mix), BF16) + _nbytes((d_mix, tn), BF16) + 2 * _nbytes((tm, tn), F32)
                   + _nbytes((tm, tn), BF16) + _nbytes((SUBLANES, tn), F32))
    scratch_bytes = _nbytes((tm, d), F32) + 2 * _nbytes((tm, LANES), F32)
    kern = functools.partial(_out_proj_kernel, n_rows=n_rows, d_model=d)
    return pl.pallas_call(
        kern,
        grid=(n_rows + 1, n_col),
        in_specs=[
            pl.BlockSpec((tm, d_mix), cur),
            pl.BlockSpec((d_mix, tn), wcol),
            pl.BlockSpec((tm, tn), prev),
            pl.BlockSpec((1, tn), ncol),
        ],
        out_specs=[pl.BlockSpec((tm, tn), prev), pl.BlockSpec((tm, tn), prev)],
        out_shape=[jax.ShapeDtypeStruct((m, d), F32), jax.ShapeDtypeStruct((m, d), BF16)],
        scratch_shapes=[pltpu.VMEM((n_col, tm, tn), F32), pltpu.VMEM((tm, 1), F32), pltpu.VMEM((tm, 1), F32)],
        compiler_params=pltpu.CompilerParams(
            dimension_semantics=("arbitrary", "arbitrary"),
            vmem_limit_bytes=_vmem_limit(block_bytes, scratch_bytes, 2 * _nbytes((tm, tn), F32))),
        name="out_proj",
    )(a, w, x, norm_w.reshape(1, d))


def _ple_kernel(x1b_ref, wg_ref, p_ref, wp_ref, x1_ref, nw_ref, y_ref, ge_ref, ssq_ref, inv_ref, *, n_rows, d_model):
    i = pl.program_id(0)
    j = pl.program_id(1)

    @pl.when(j == 0)
    def _():
        @pl.when(i > 0)
        def _():
            inv_ref[...] = lax.rsqrt(ssq_ref[...] * (1.0 / d_model) + EPS)

        ssq_ref[...] = jnp.zeros_like(ssq_ref)

    @pl.when(i > 0)
    def _finish():
        y_ref[...] = x1_ref[...] + ge_ref[j] * inv_ref[...] * nw_ref[...]

    @pl.when(i < n_rows)
    def _matmul():
        gate = _sigmoid(_dot(x1b_ref[...], wg_ref[...]))
        emb = _dot(p_ref[...].astype(BF16), wp_ref[...])
        ge = gate * emb
        ge_ref[j] = ge
        ssq_ref[...] += jnp.sum(ge * ge, axis=-1, keepdims=True)


def _ple(x1b, w_gate, p, w_proj, x1, norm_w, *, tm, tn):
    m, d = x1.shape
    dp = p.shape[1]
    assert m % tm == 0 and d % tn == 0
    n_col = d // tn
    n_rows = m // tm
    cur, wcol, prev, ncol = _lagged_maps(n_rows, n_col)
    block_bytes = (_nbytes((tm, d), BF16) + _nbytes((d, tn), BF16) + _nbytes((tm, dp), F32) + _nbytes((dp, tn), BF16)
                   + 2 * _nbytes((tm, tn), F32) + _nbytes((SUBLANES, tn), F32))
    scratch_bytes = _nbytes((tm, d), F32) + 2 * _nbytes((tm, LANES), F32)
    kern = functools.partial(_ple_kernel, n_rows=n_rows, d_model=d)
    return pl.pallas_call(
        kern,
        grid=(n_rows + 1, n_col),
        in_specs=[
            pl.BlockSpec((tm, d), cur),
            pl.BlockSpec((d, tn), wcol),
            pl.BlockSpec((tm, dp), cur),
            pl.BlockSpec((dp, tn), wcol),
            pl.BlockSpec((tm, tn), prev),
            pl.BlockSpec((1, tn), ncol),
        ],
        out_specs=pl.BlockSpec((tm, tn), prev),
        out_shape=jax.ShapeDtypeStruct((m, d), F32),
        scratch_shapes=[pltpu.VMEM((n_col, tm, tn), F32), pltpu.VMEM((tm, 1), F32), pltpu.VMEM((tm, 1), F32)],
        compiler_params=pltpu.CompilerParams(
            dimension_semantics=("arbitrary", "arbitrary"),
            vmem_limit_bytes=_vmem_limit(block_bytes, scratch_bytes, 3 * _nbytes((tm, tn), F32))),
        name="ple",
    )(x1b, w_gate, p, w_proj, x1, norm_w.reshape(1, d))


def _row_tile(m, target):
    return target if m % target == 0 else m


def _trunk_layer(x, p, k_past, v_past, conv_past, ssd_past, wts, *, d_sb, d_ssd, conv_ch):
    (norm_pre, norm_post, w_main, w_dt, conv_w, conv_b, dt_bias, a_log, d_skip, sb_norm, ssd_norm,
     w_out, w_gate, w_proj, ple_norm) = wts
    b, t, d = x.shape
    m = b * t
    xf = x.reshape(m, d)
    tm = _row_tile(m, 1024)
    h = _prenorm(xf, norm_pre.reshape(1, d), tm=_row_tile(m, 256))
    q, k, v, g, xbc, z, dt_raw = _in_proj(h, w_main, w_dt, d_sb=d_sb, conv_ch=conv_ch, d_ssd=d_ssd, tm=tm)
    tq = 256 if t % 256 == 0 else t
    kp = None if k_past is None else k_past.reshape(b, k_past.shape[1], d_sb)
    vp = None if v_past is None else v_past.reshape(b, v_past.shape[1], d_sb)
    o = _attention(q.reshape(b, t, d_sb), k.reshape(b, t, d_sb), v.reshape(b, t, d_sb), kp, vp,
                   tq=tq, tp=256, nh=2)
    chunk = 128 if t % 128 == 0 else t
    y, ssd_new = _ssd(xbc.reshape(b, t, conv_ch), dt_raw.reshape(b, t, LANES), conv_past, ssd_past,
                      conv_w, conv_b, dt_bias, a_log, d_skip, d_ssd=d_ssd, chunk=chunk)
    a = _gate_norm(o.reshape(m, d_sb), g, y.reshape(m, d_ssd), z, sb_norm, ssd_norm, tm=_row_tile(m, 256))
    x1, x1b = _out_proj(a, w_out, xf, norm_post, tm=tm, tn=512 if d % 512 == 0 else d)
    yo = _ple(x1b, w_gate, p.reshape(m, p.shape[-1]), w_proj, x1, ple_norm, tm=tm, tn=512 if d % 512 == 0 else d)
    heads = d_sb // SB_HEAD_DIM
    conv_new = xbc.reshape(b, t, conv_ch)[:, t - (CONV_WIDTH - 1):, :]
    return (yo.reshape(b, t, d), k.reshape(b, t, heads, SB_HEAD_DIM), v.reshape(b, t, heads, SB_HEAD_DIM),
            conv_new, ssd_new)


def kernel(x_prompt, x_sample, cache_k, cache_v, state_conv, state_ssd, p_prompt, p_sample, norm_pre, norm_post,
           w_in, conv_w, conv_b, dt_bias, a_log, d_skip, sb_norm, ssd_norm, w_out, w_ple_gate, w_ple_proj, ple_norm):
    depth = norm_pre.shape[0]
    ssd_heads = dt_bias.shape[1]
    d_ssd = ssd_heads * SSD_HEAD_DIM
    d_sb = sb_norm.shape[1]
    conv_ch = conv_w.shape[2]
    n_main = 4 * d_sb + conv_ch + d_ssd
    bp = x_prompt.shape[0]
    yp, ys = x_prompt, x_sample
    outs = [[] for _ in range(8)]
    for i in range(depth):
        w_main = w_in[i][:, :n_main].astype(BF16)
        w_dt = jnp.pad(w_in[i][:, n_main:], ((0, 0), (0, LANES - ssd_heads))).astype(BF16)
        wts = (norm_pre[i], norm_post[i], w_main, w_dt, conv_w[i], conv_b[i], dt_bias[i], a_log[i], d_skip[i],
               sb_norm[i], ssd_norm[i], w_out[i].astype(BF16), w_ple_gate[i].astype(BF16),
               w_ple_proj[i].astype(BF16), ple_norm[i])
        dims = dict(d_sb=d_sb, d_ssd=d_ssd, conv_ch=conv_ch)
        zero_conv = jnp.zeros((bp, CONV_WIDTH - 1, conv_ch), F32)
        zero_ssd = jnp.zeros((bp, ssd_heads, SSD_HEAD_DIM, SSD_STATE), F32)
        yp, kp, vp, cp, sp = _trunk_layer(yp, p_prompt[i], None, None, zero_conv, zero_ssd, wts, **dims)
        ys, ks, vs, cs, ss = _trunk_layer(ys, p_sample[i], cache_k[i], cache_v[i], state_conv[i], state_ssd[i],
                                          wts, **dims)
        for lst, val in zip(outs, (kp, vp, cp, sp, ks, vs, cs, ss)):
            lst.append(val)
    stacked = [jnp.stack(lst) for lst in outs]
    return (yp, ys, *stacked)
```

```python
import functools
import math

import jax
import jax.numpy as jnp
from jax import lax
from jax.experimental import pallas as pl
from jax.experimental.pallas import tpu as pltpu

F32 = jnp.float32
BF16 = jnp.bfloat16

EPS = 1e-6
SB_HEAD_DIM = 128
SSD_HEAD_DIM = 64
SSD_STATE = 128
SSD_GROUPS = 8
CONV_WIDTH = 4
LANES = 128
SUBLANES = 8
V7X_VMEM_BYTES = 64 * 2**20


def _vmem_limit(block_bytes, scratch_bytes, temp_bytes):
    need = 2 * block_bytes + scratch_bytes + temp_bytes
    return int(min(need, V7X_VMEM_BYTES - 2 * 2**20))


def _nbytes(shape, dtype):
    return math.prod(shape) * jnp.dtype(dtype).itemsize


def _sigmoid(x):
    return 1.0 / (1.0 + jnp.exp(-x))


def _silu(x):
    return x * _sigmoid(x)


def _split_bf16(x, parts):
    out = []
    r = x
    for p in range(parts):
        h = r.astype(BF16)
        out.append(h)
        if p + 1 < parts:
            r = r - h.astype(F32)
    return out


def _dot(a, b):
    return jnp.dot(a, b, preferred_element_type=F32)


def _dot_split(x, m, parts):
    acc = None
    for h in _split_bf16(x, parts):
        t = _dot(h, m)
        acc = t if acc is None else acc + t
    return acc


def _prenorm_kernel(x_ref, nw_ref, h_ref, *, row_chunk):
    tm = x_ref.shape[0]

    def body(r, carry):
        rows = pl.ds(pl.multiple_of(r * row_chunk, row_chunk), row_chunk)
        xf = x_ref[rows, :]
        ms = jnp.mean(xf * xf, axis=-1, keepdims=True)
        h_ref[rows, :] = (xf * lax.rsqrt(ms + EPS) * nw_ref[...]).astype(BF16)
        return carry

    lax.fori_loop(0, tm // row_chunk, body, 0, unroll=2)


def _prenorm(x, norm_w, *, tm):
    m, d = x.shape
    assert m % tm == 0
    block_bytes = _nbytes((tm, d), F32) + _nbytes((tm, d), BF16)
    return pl.pallas_call(
        functools.partial(_prenorm_kernel, row_chunk=32),
        grid=(m // tm,),
        in_specs=[pl.BlockSpec((tm, d), lambda i: (i, 0)), pl.BlockSpec((1, d), lambda i: (0, 0))],
        out_specs=pl.BlockSpec((tm, d), lambda i: (i, 0)),
        out_shape=jax.ShapeDtypeStruct((m, d), BF16),
        compiler_params=pltpu.CompilerParams(
            dimension_semantics=("arbitrary",),
            vmem_limit_bytes=_vmem_limit(block_bytes, 0, 8 * _nbytes((32, d), F32))),
        name="prenorm",
    )(x, norm_w)


def _in_proj_kernel(h_ref, w_ref, wdt_ref, q_ref, k_ref, v_ref, g_ref, xbc_ref, z_ref, dt_ref, *, bounds, q_scale):
    j = pl.program_id(1)

    def segment(lo, hi, out_ref, scale):
        @pl.when((j >= lo) & (j < hi))
        def _():
            acc = _dot(h_ref[...], w_ref[...])
            if scale is not None:
                acc = acc * scale
            out_ref[...] = acc.astype(out_ref.dtype)

    outs = (q_ref, k_ref, v_ref, g_ref, xbc_ref, z_ref)
    for s, out_ref in enumerate(outs):
        segment(bounds[s], bounds[s + 1], out_ref, q_scale if s == 0 else None)

    @pl.when(j == bounds[-1])
    def _dt():
        dt_ref[...] = _dot(h_ref[...], wdt_ref[...])


def _in_proj(h, w_main, w_dt, *, d_sb, conv_ch, d_ssd, tm):
    m, d = h.shape
    tn = math.gcd(math.gcd(d_sb, 512), math.gcd(conv_ch, d_ssd))
    widths = (d_sb, d_sb, d_sb, d_sb, conv_ch, d_ssd)
    bounds = [0]
    for w in widths:
        bounds.append(bounds[-1] + w // tn)
    bounds = tuple(bounds)
    n_tiles = bounds[-1]
    assert w_main.shape == (d, n_tiles * tn) and w_dt.shape == (d, LANES) and m % tm == 0

    def out_map(lo, hi):
        return lambda i, j: (i, jnp.clip(j - lo, 0, hi - lo - 1))

    out_shapes = [jax.ShapeDtypeStruct((m, d_sb), BF16)]
    out_shapes += [jax.ShapeDtypeStruct((m, wd), F32) for wd in widths[1:]]
    out_shapes.append(jax.ShapeDtypeStruct((m, LANES), F32))
    out_specs = [pl.BlockSpec((tm, tn), out_map(bounds[s], bounds[s + 1])) for s in range(len(widths))]
    out_specs.append(pl.BlockSpec((tm, LANES), lambda i, j: (i, 0)))

    block_bytes = (_nbytes((tm, d), BF16) + _nbytes((d, tn), BF16) + _nbytes((d, LANES), BF16)
                   + _nbytes((tm, tn), BF16) + 5 * _nbytes((tm, tn), F32) + _nbytes((tm, LANES), F32))
    temp_bytes = 2 * _nbytes((tm, tn), F32)
    kern = functools.partial(_in_proj_kernel, bounds=bounds, q_scale=SB_HEAD_DIM ** -0.5)
    return pl.pallas_call(
        kern,
        grid=(m // tm, n_tiles + 1),
        in_specs=[
            pl.BlockSpec((tm, d), lambda i, j: (i, 0)),
            pl.BlockSpec((d, tn), lambda i, j: (0, jnp.minimum(j, n_tiles - 1))),
            pl.BlockSpec((d, LANES), lambda i, j: (0, 0)),
        ],
        out_specs=out_specs,
        out_shape=out_shapes,
        compiler_params=pltpu.CompilerParams(
            dimension_semantics=("arbitrary", "arbitrary"),
            vmem_limit_bytes=_vmem_limit(block_bytes, 0, temp_bytes)),
        name="in_proj",
    )(h, w_main, w_dt)


def _attn_kernel(*refs, t, tk, tp, n_past):
    if n_past:
        q_ref, k_ref, v_ref, kp_ref, vp_ref, u_ref, o_ref, carry_ref = refs
    else:
        q_ref, k_ref, v_ref, u_ref, o_ref, carry_ref = refs
    row = lax.broadcasted_iota(jnp.int32, (tk, tk), 0)
    col = lax.broadcasted_iota(jnp.int32, (tk, tk), 1)
    causal = col < row

    def strip(q, kb, vb, u, r0, n_diag):
        z = lax.dot_general(q, kb.astype(BF16), (((1,), (1,)), ((), ())), preferred_element_type=F32)
        sp = jnp.log(1.0 + jnp.exp(-jnp.abs(z)))
        log_stop = jnp.minimum(z, 0.0) - sp
        log_stay = log_stop - z
        n_rows = z.shape[0]
        has_rest = n_rows > n_diag
        if n_diag:
            diag_stay = jnp.where(causal, log_stay[:n_diag], 0.0)
            log_stay = jnp.concatenate([diag_stay, log_stay[n_diag:]], axis=0) if has_rest else diag_stay
        later = _dot(log_stay.astype(BF16), u)
        total = jnp.sum(log_stay, axis=-1, keepdims=True)
        parts = []
        if n_diag:
            att_diag = jnp.where(causal, jnp.exp(log_stop[:n_diag] + later[:n_diag]), 0.0)
            parts.append(att_diag)
        if has_rest:
            carry = carry_ref[r0 + n_diag:r0 + n_rows, :]
            parts.append(jnp.exp(log_stop[n_diag:] + later[n_diag:] + carry))
        att = jnp.concatenate(parts, axis=0) if len(parts) > 1 else parts[0]
        pv = _dot(att.astype(BF16), vb.astype(BF16))
        if n_diag:
            o_ref[r0:r0 + n_diag, :] = pv[:n_diag]
            carry_ref[r0:r0 + n_diag, :] = total[:n_diag]
        if has_rest:
            o_ref[r0 + n_diag:r0 + n_rows, :] += pv[n_diag:]
            carry_ref[r0 + n_diag:r0 + n_rows, :] = carry + total[n_diag:]

    u_new = u_ref[:tk, :tk]
    for c in reversed(range(t // tk)):
        r0 = c * tk
        strip(q_ref[r0:, :], k_ref[r0:r0 + tk, :], v_ref[r0:r0 + tk, :], u_new, r0, tk)
    if n_past:
        u_past = u_ref[:tp, :tp]
        for c in reversed(range(n_past)):
            strip(q_ref[...], kp_ref[c * tp:(c + 1) * tp, :], vp_ref[c * tp:(c + 1) * tp, :], u_past, 0, 0)


def _attention(q, k, v, k_past, v_past, *, tk, tp):
    b, t, hd = k.shape
    d = SB_HEAD_DIM
    heads = hd // d
    p = 0 if k_past is None else k_past.shape[1]
    assert t % tk == 0 and p % tp == 0
    n_past = p // tp
    tu = max(tk, tp) if n_past else tk
    ridx = lax.broadcasted_iota(jnp.int32, (tu, tu), 0)
    cidx = lax.broadcasted_iota(jnp.int32, (tu, tu), 1)
    u = (ridx > cidx).astype(BF16)

    head_spec = pl.BlockSpec((None, t, d), lambda bi, h: (bi, 0, h))
    in_specs = [head_spec, head_spec, head_spec]
    args = [q, k, v]
    block_bytes = _nbytes((t, d), BF16) + 3 * _nbytes((t, d), F32) + _nbytes((tu, tu), BF16)
    if n_past:
        past_spec = pl.BlockSpec((None, p, d), lambda bi, h: (bi, 0, h))
        in_specs += [past_spec, past_spec]
        args += [k_past, v_past]
        block_bytes += 2 * _nbytes((p, d), F32)
    in_specs.append(pl.BlockSpec((tu, tu), lambda bi, h: (0, 0)))
    args.append(u)
    scratch_bytes = _nbytes((t, LANES), F32)
    temp_bytes = 8 * _nbytes((t, tu), F32)
    kern = functools.partial(_attn_kernel, t=t, tk=tk, tp=tp, n_past=n_past)
    return pl.pallas_call(
        kern,
        grid=(b, heads),
        in_specs=in_specs,
        out_specs=head_spec,
        out_shape=jax.ShapeDtypeStruct((b, t, hd), F32),
        scratch_shapes=[pltpu.VMEM((t, 1), F32)],
        compiler_params=pltpu.CompilerParams(
            dimension_semantics=("arbitrary", "arbitrary"),
            vmem_limit_bytes=_vmem_limit(block_bytes, scratch_bytes, temp_bytes)),
        name="stick_breaking",
    )(*args)


def _ssd_kernel(xbc_ref, dt_ref, convp_ref, statep_ref, cw_ref, cb_ref, dtb_ref, alog_ref, dskip_ref, e_ref, tri_ref,
                y_ref, state_ref, ext_ref, act_ref, ht_ref, *, chunk, d_ssd, col_slab):
    c = pl.program_id(1)
    n_chunks = pl.num_programs(1)
    ch = xbc_ref.shape[-1]
    gn = SSD_GROUPS * SSD_STATE
    hpg = d_ssd // SSD_HEAD_DIM // SSD_GROUPS
    gw = hpg * SSD_HEAD_DIM
    pad = SUBLANES

    @pl.when(c == 0)
    def _init():
        ext_ref[0:pad, :] = jnp.zeros((pad, ch), F32)
        ext_ref[pad - (CONV_WIDTH - 1):pad, :] = convp_ref[...]
        for g in range(SSD_GROUPS):
            ht_ref[g] = statep_ref[g].T

    @pl.when(c > 0)
    def _shift():
        ext_ref[0:pad, :] = ext_ref[chunk:chunk + pad, :]

    ext_ref[pad:pad + chunk, :] = xbc_ref[...]

    for s in range(ch // col_slab):
        cols = slice(s * col_slab, (s + 1) * col_slab)
        acc = cb_ref[:, cols]
        for jj in range(CONV_WIDTH):
            lo = pad - (CONV_WIDTH - 1) + jj
            acc = acc + cw_ref[jj:jj + 1, cols] * ext_ref[lo:lo + chunk, cols]
        act_ref[:, cols] = _silu(acc)

    dt_in = dt_ref[...] + dtb_ref[...]
    dt = jnp.maximum(dt_in, 0.0) + jnp.log(1.0 + jnp.exp(-jnp.abs(dt_in)))
    da = dt * (-jnp.exp(alog_ref[...]))
    tri = tri_ref[...]
    acum = _dot_tri(tri, da)
    last = acum[chunk - 1:chunk, :]
    acum_t = acum.T
    dt_t = dt.T
    row = lax.broadcasted_iota(jnp.int32, (chunk, chunk), 0)
    col = lax.broadcasted_iota(jnp.int32, (chunk, chunk), 1)
    causal = col <= row

    e = e_ref[...]
    de_e = _dot_split(jnp.exp(last - acum) * dt, e, 2)
    ea_e = _dot_split(jnp.exp(acum), e, 2)
    bd_e = _dot_split(jnp.broadcast_to(jnp.exp(last), (SUBLANES, LANES)), e, 2)[0:1, :]

    lane = lax.broadcasted_iota(jnp.int32, (chunk, gw), 1)
    for g in range(SSD_GROUPS):
        gs = slice(g * gw, (g + 1) * gw)
        xg = act_ref[:, gs]
        bg = act_ref[:, d_ssd + g * SSD_STATE:d_ssd + (g + 1) * SSD_STATE].astype(BF16)
        cg = act_ref[:, d_ssd + gn + g * SSD_STATE:d_ssd + gn + (g + 1) * SSD_STATE].astype(BF16)
        cb = lax.dot_general(cg, bg, (((1,), (1,)), ((), ())), preferred_element_type=F32)
        xgb = xg.astype(BF16)
        y_intra = None
        for hh in range(hpg):
            h = g * hpg + hh
            diff = acum[:, h:h + 1] - acum_t[h:h + 1, :]
            decay = jnp.exp(jnp.where(causal, diff, -jnp.inf))
            w = (cb * decay * dt_t[h:h + 1, :]).astype(BF16)
            in_head = (lane >= hh * SSD_HEAD_DIM) & (lane < (hh + 1) * SSD_HEAD_DIM)
            t = _dot(w, jnp.where(in_head, xgb, jnp.zeros_like(xgb)))
            y_intra = t if y_intra is None else y_intra + t
        h_prev = ht_ref[g]
        y_inter = _dot(cg, h_prev.astype(BF16)) * ea_e[:, gs]
        xw = (xg * de_e[:, gs]).astype(BF16)
        s_t = lax.dot_general(bg, xw, (((0,), (0,)), ((), ())), preferred_element_type=F32)
        ht_ref[g] = h_prev * bd_e[:, gs] + s_t
        y_ref[:, gs] = y_intra + y_inter + dskip_ref[:, gs] * xg

    @pl.when(c == n_chunks - 1)
    def _final():
        for g in range(SSD_GROUPS):
            state_ref[g] = ht_ref[g].T


def _dot_tri(tri, x):
    acc = None
    for h in _split_bf16(x, 3):
        t = _dot(tri, h)
        acc = t if acc is None else acc + t
    return acc


def _ssd(xbc, dt_raw, conv_past, state_past, conv_w, conv_b, dt_bias, a_log, d_skip, *, d_ssd, chunk):
    b, t, ch = xbc.shape
    heads = d_ssd // SSD_HEAD_DIM
    hpg = heads // SSD_GROUPS
    gw = hpg * SSD_HEAD_DIM
    assert t % chunk == 0 and heads <= LANES
    n_chunks = t // chunk
    pad_h = LANES - heads
    dtb = jnp.pad(dt_bias.astype(F32), (0, pad_h)).reshape(1, LANES)
    alog = jnp.pad(a_log.astype(F32), (0, pad_h)).reshape(1, LANES)
    dskip = jnp.repeat(d_skip.astype(F32), SSD_HEAD_DIM).reshape(1, d_ssd)
    hidx = lax.broadcasted_iota(jnp.int32, (LANES, d_ssd), 0)
    lidx = lax.broadcasted_iota(jnp.int32, (LANES, d_ssd), 1)
    expand = (lidx // SSD_HEAD_DIM == hidx).astype(BF16)
    ridx = lax.broadcasted_iota(jnp.int32, (chunk, chunk), 0)
    cidx = lax.broadcasted_iota(jnp.int32, (chunk, chunk), 1)
    tri = (cidx <= ridx).astype(BF16)
    state_g = state_past.reshape(b, SSD_GROUPS, gw, SSD_STATE)

    full = lambda shape: pl.BlockSpec(shape, lambda bi, c: (0,) * len(shape))
    col_slab = 512
    block_bytes = (_nbytes((chunk, ch), F32) + _nbytes((chunk, LANES), F32) + _nbytes((SUBLANES, ch), F32)
                   + 2 * _nbytes((SSD_GROUPS, gw, SSD_STATE), F32) + _nbytes((CONV_WIDTH + 1, ch), F32) * 2
                   + _nbytes((LANES, d_ssd), BF16) + _nbytes((chunk, chunk), BF16) + _nbytes((chunk, d_ssd), F32))
    scratch_bytes = (_nbytes((chunk + SUBLANES, ch), F32) + _nbytes((chunk, ch), F32)
                     + _nbytes((SSD_GROUPS, SSD_STATE, gw), F32))
    temp_bytes = 4 * _nbytes((chunk, d_ssd), F32) + 16 * _nbytes((chunk, max(chunk, LANES)), F32)
    kern = functools.partial(_ssd_kernel, chunk=chunk, d_ssd=d_ssd, col_slab=col_slab)
    y, state = pl.pallas_call(
        kern,
        grid=(b, n_chunks),
        in_specs=[
            pl.BlockSpec((None, chunk, ch), lambda bi, c: (bi, c, 0)),
            pl.BlockSpec((None, chunk, LANES), lambda bi, c: (bi, c, 0)),
            pl.BlockSpec((None, CONV_WIDTH - 1, ch), lambda bi, c: (bi, 0, 0)),
            pl.BlockSpec((None, SSD_GROUPS, gw, SSD_STATE), lambda bi, c: (bi, 0, 0, 0)),
            full((CONV_WIDTH, ch)), full((1, ch)), full((1, LANES)), full((1, LANES)), full((1, d_ssd)),
            full((LANES, d_ssd)), full((chunk, chunk)),
        ],
        out_specs=[
            pl.BlockSpec((None, chunk, d_ssd), lambda bi, c: (bi, c, 0)),
            pl.BlockSpec((None, SSD_GROUPS, gw, SSD_STATE), lambda bi, c: (bi, 0, 0, 0)),
        ],
        out_shape=[jax.ShapeDtypeStruct((b, t, d_ssd), F32),
                   jax.ShapeDtypeStruct((b, SSD_GROUPS, gw, SSD_STATE), F32)],
        scratch_shapes=[pltpu.VMEM((chunk + SUBLANES, ch), F32), pltpu.VMEM((chunk, ch), F32),
                        pltpu.VMEM((SSD_GROUPS, SSD_STATE, gw), F32)],
        compiler_params=pltpu.CompilerParams(
            dimension_semantics=("arbitrary", "arbitrary"),
            vmem_limit_bytes=_vmem_limit(block_bytes, scratch_bytes, temp_bytes)),
        name="conv_ssd",
    )(xbc, dt_raw, conv_past, state_g, conv_w, conv_b.reshape(1, ch), dtb, alog, dskip, expand, tri)
    return y, state.reshape(b, heads, SSD_HEAD_DIM, SSD_STATE)


def _gate_norm_kernel(o_ref, g_ref, y_ref, z_ref, nsb_ref, nssd_ref, a_ref, *, d_sb, row_chunk):
    tm = o_ref.shape[0]

    def body(r, carry):
        rows = pl.ds(pl.multiple_of(r * row_chunk, row_chunk), row_chunk)
        u = o_ref[rows, :] * _silu(g_ref[rows, :])
        ms = jnp.mean(u * u, axis=-1, keepdims=True)
        a_ref[rows, 0:d_sb] = (u * lax.rsqrt(ms + EPS) * nsb_ref[...]).astype(BF16)
        w = y_ref[rows, :] * _silu(z_ref[rows, :])
        ms2 = jnp.mean(w * w, axis=-1, keepdims=True)
        a_ref[rows, d_sb:] = (w * lax.rsqrt(ms2 + EPS) * nssd_ref[...]).astype(BF16)
        return carry

    lax.fori_loop(0, tm // row_chunk, body, 0)


def _gate_norm(o, g, y, z, sb_norm, ssd_norm, *, tm):
    m, d_sb = o.shape
    d_ssd = y.shape[1]
    assert m % tm == 0
    block_bytes = 2 * _nbytes((tm, d_sb), F32) + 2 * _nbytes((tm, d_ssd), F32) + _nbytes((tm, d_sb + d_ssd), BF16)
    kern = functools.partial(_gate_norm_kernel, d_sb=d_sb, row_chunk=16)
    return pl.pallas_call(
        kern,
        grid=(m // tm,),
        in_specs=[
            pl.BlockSpec((tm, d_sb), lambda i: (i, 0)), pl.BlockSpec((tm, d_sb), lambda i: (i, 0)),
            pl.BlockSpec((tm, d_ssd), lambda i: (i, 0)), pl.BlockSpec((tm, d_ssd), lambda i: (i, 0)),
            pl.BlockSpec((1, d_sb), lambda i: (0, 0)), pl.BlockSpec((1, d_ssd), lambda i: (0, 0)),
        ],
        out_specs=pl.BlockSpec((tm, d_sb + d_ssd), lambda i: (i, 0)),
        out_shape=jax.ShapeDtypeStruct((m, d_sb + d_ssd), BF16),
        compiler_params=pltpu.CompilerParams(
            dimension_semantics=("arbitrary",),
            vmem_limit_bytes=_vmem_limit(block_bytes, 0, 8 * _nbytes((16, d_sb + d_ssd), F32))),
        name="gate_norm",
    )(o, g, y, z, sb_norm.reshape(1, d_sb), ssd_norm.reshape(1, d_ssd))


def _lagged_maps(n_rows, n_col):
    cur = lambda i, j: (jnp.minimum(i, n_rows - 1), 0)
    wcol = lambda i, j: (0, jnp.where(i < n_rows, j, n_col - 1))
    prev = lambda i, j: (jnp.maximum(i - 1, 0), jnp.where(i > 0, j, 0))
    ncol = lambda i, j: (0, jnp.where(i > 0, j, 0))
    return cur, wcol, prev, ncol


def _out_proj_kernel(a_ref, w_ref, x_ref, nw_ref, x1_ref, x1b_ref, m_ref, ssq_ref, inv_ref, *, n_rows, d_model):
    i = pl.program_id(0)
    j = pl.program_id(1)

    @pl.when(j == 0)
    def _():
        @pl.when(i > 0)
        def _():
            inv_ref[...] = lax.rsqrt(ssq_ref[...] * (1.0 / d_model) + EPS)

        ssq_ref[...] = jnp.zeros_like(ssq_ref)

    @pl.when(i > 0)
    def _finish():
        x1 = x_ref[...] + m_ref[j] * inv_ref[...] * nw_ref[...]
        x1_ref[...] = x1
        x1b_ref[...] = x1.astype(BF16)

    @pl.when(i < n_rows)
    def _matmul():
        mt = _dot(a_ref[...], w_ref[...])
        m_ref[j] = mt
        ssq_ref[...] += jnp.sum(mt * mt, axis=-1, keepdims=True)


def _out_proj(a, w, x, norm_w, *, tm, tn):
    m, d_mix = a.shape
    d = w.shape[1]
    assert m % tm == 0 and d % tn == 0
    n_col = d // tn
    n_rows = m // tm
    cur, wcol, prev, ncol = _lagged_maps(n_rows, n_col)
    block_bytes = (_nbytes((tm, d_mix), BF16) + _nbytes((d_mix, tn), BF16) + 2 * _nbytes((tm, tn), F32)
                   + _nbytes((tm, tn), BF16) + _nbytes((SUBLANES, tn), F32))
    scratch_bytes = _nbytes((tm, d), F32) + 2 * _nbytes((tm, LANES), F32)
    kern = functools.partial(_out_proj_kernel, n_rows=n_rows, d_model=d)
    return pl.pallas_call(
        kern,
        grid=(n_rows + 1, n_col),
        in_specs=[
            pl.BlockSpec((tm, d_mix), cur),
            pl.BlockSpec((d_mix, tn), wcol),
            pl.BlockSpec((tm, tn), prev),
            pl.BlockSpec((1, tn), ncol),
        ],
        out_specs=[pl.BlockSpec((tm, tn), prev), pl.BlockSpec((tm, tn), prev)],
        out_shape=[jax.ShapeDtypeStruct((m, d), F32), jax.ShapeDtypeStruct((m, d), BF16)],
        scratch_shapes=[pltpu.VMEM((n_col, tm, tn), F32), pltpu.VMEM((tm, 1), F32), pltpu.VMEM((tm, 1), F32)],
        compiler_params=pltpu.CompilerParams(
            dimension_semantics=("arbitrary", "arbitrary"),
            vmem_limit_bytes=_vmem_limit(block_bytes, scratch_bytes, 2 * _nbytes((tm, tn), F32))),
        name="out_proj",
    )(a, w, x, norm_w.reshape(1, d))


def _ple_kernel(x1b_ref, wg_ref, p_ref, wp_ref, x1_ref, nw_ref, y_ref, ge_ref, ssq_ref, inv_ref, *, n_rows, d_model):
    i = pl.program_id(0)
    j = pl.program_id(1)

    @pl.when(j == 0)
    def _():
        @pl.when(i > 0)
        def _():
            inv_ref[...] = lax.rsqrt(ssq_ref[...] * (1.0 / d_model) + EPS)

        ssq_ref[...] = jnp.zeros_like(ssq_ref)

    @pl.when(i > 0)
    def _finish():
        y_ref[...] = x1_ref[...] + ge_ref[j] * inv_ref[...] * nw_ref[...]

    @pl.when(i < n_rows)
    def _matmul():
        gate = _sigmoid(_dot(x1b_ref[...], wg_ref[...]))
        emb = _dot(p_ref[...].astype(BF16), wp_ref[...])
        ge = gate * emb
        ge_ref[j] = ge
        ssq_ref[...] += jnp.sum(ge * ge, axis=-1, keepdims=True)


def _ple(x1b, w_gate, p, w_proj, x1, norm_w, *, tm, tn):
    m, d = x1.shape
    dp = p.shape[1]
    assert m % tm == 0 and d % tn == 0
    n_col = d // tn
    n_rows = m // tm
    cur, wcol, prev, ncol = _lagged_maps(n_rows, n_col)
    block_bytes = (_nbytes((tm, d), BF16) + _nbytes((d, tn), BF16) + _nbytes((tm, dp), F32) + _nbytes((dp, tn), BF16)
                   + 2 * _nbytes((tm, tn), F32) + _nbytes((SUBLANES, tn), F32))
    scratch_bytes = _nbytes((tm, d), F32) + 2 * _nbytes((tm, LANES), F32)
    kern = functools.partial(_ple_kernel, n_rows=n_rows, d_model=d)
    return pl.pallas_call(
        kern,
        grid=(n_rows + 1, n_col),
        in_specs=[
            pl.BlockSpec((tm, d), cur),
            pl.BlockSpec((d, tn), wcol),
            pl.BlockSpec((tm, dp), cur),
            pl.BlockSpec((dp, tn), wcol),
            pl.BlockSpec((tm, tn), prev),
            pl.BlockSpec((1, tn), ncol),
        ],
        out_specs=pl.BlockSpec((tm, tn), prev),
        out_shape=jax.ShapeDtypeStruct((m, d), F32),
        scratch_shapes=[pltpu.VMEM((n_col, tm, tn), F32), pltpu.VMEM((tm, 1), F32), pltpu.VMEM((tm, 1), F32)],
        compiler_params=pltpu.CompilerParams(
            dimension_semantics=("arbitrary", "arbitrary"),
            vmem_limit_bytes=_vmem_limit(block_bytes, scratch_bytes, 3 * _nbytes((tm, tn), F32))),
        name="ple",
    )(x1b, w_gate, p, w_proj, x1, norm_w.reshape(1, d))


def _row_tile(m, target):
    return target if m % target == 0 else m


def _trunk_layer(x, p, k_past, v_past, conv_past, ssd_past, wts, *, d_sb, d_ssd, conv_ch):
    (norm_pre, norm_post, w_main, w_dt, conv_w, conv_b, dt_bias, a_log, d_skip, sb_norm, ssd_norm,
     w_out, w_gate, w_proj, ple_norm) = wts
    b, t, d = x.shape
    m = b * t
    xf = x.reshape(m, d)
    tm = _row_tile(m, 1024)
    h = _prenorm(xf, norm_pre.reshape(1, d), tm=_row_tile(m, 256))
    q, k, v, g, xbc, z, dt_raw = _in_proj(h, w_main, w_dt, d_sb=d_sb, conv_ch=conv_ch, d_ssd=d_ssd, tm=tm)
    tk = 256 if t % 256 == 0 else t
    kp = None if k_past is None else k_past.reshape(b, k_past.shape[1], d_sb)
    vp = None if v_past is None else v_past.reshape(b, v_past.shape[1], d_sb)
    o = _attention(q.reshape(b, t, d_sb), k.reshape(b, t, d_sb), v.reshape(b, t, d_sb), kp, vp, tk=tk, tp=256)
    chunk = 128 if t % 128 == 0 else t
    y, ssd_new = _ssd(xbc.reshape(b, t, conv_ch), dt_raw.reshape(b, t, LANES), conv_past, ssd_past,
                      conv_w, conv_b, dt_bias, a_log, d_skip, d_ssd=d_ssd, chunk=chunk)
    a = _gate_norm(o.reshape(m, d_sb), g, y.reshape(m, d_ssd), z, sb_norm, ssd_norm, tm=_row_tile(m, 256))
    x1, x1b = _out_proj(a, w_out, xf, norm_post, tm=tm, tn=512 if d % 512 == 0 else d)
    yo = _ple(x1b, w_gate, p.reshape(m, p.shape[-1]), w_proj, x1, ple_norm, tm=tm, tn=512 if d % 512 == 0 else d)
    heads = d_sb // SB_HEAD_DIM
    conv_new = xbc.reshape(b, t, conv_ch)[:, t - (CONV_WIDTH - 1):, :]
    return (yo.reshape(b, t, d), k.reshape(b, t, heads, SB_HEAD_DIM), v.reshape(b, t, heads, SB_HEAD_DIM),
            conv_new, ssd_new)


def kernel(x_prompt, x_sample, cache_k, cache_v, state_conv, state_ssd, p_prompt, p_sample, norm_pre, norm_post,
           w_in, conv_w, conv_b, dt_bias, a_log, d_skip, sb_norm, ssd_norm, w_out, w_ple_gate, w_ple_proj, ple_norm):
    depth = norm_pre.shape[0]
    ssd_heads = dt_bias.shape[1]
    d_ssd = ssd_heads * SSD_HEAD_DIM
    d_sb = sb_norm.shape[1]
    conv_ch = conv_w.shape[2]
    n_main = 4 * d_sb + conv_ch + d_ssd
    bp = x_prompt.shape[0]
    yp, ys = x_prompt, x_sample
    outs = [[] for _ in range(8)]
    for i in range(depth):
        w_main = w_in[i][:, :n_main].astype(BF16)
        w_dt = jnp.pad(w_in[i][:, n_main:], ((0, 0), (0, LANES - ssd_heads))).astype(BF16)
        wts = (norm_pre[i], norm_post[i], w_main, w_dt, conv_w[i], conv_b[i], dt_bias[i], a_log[i], d_skip[i],
               sb_norm[i], ssd_norm[i], w_out[i].astype(BF16), w_ple_gate[i].astype(BF16),
               w_ple_proj[i].astype(BF16), ple_norm[i])
        dims = dict(d_sb=d_sb, d_ssd=d_ssd, conv_ch=conv_ch)
        zero_conv = jnp.zeros((bp, CONV_WIDTH - 1, conv_ch), F32)
        zero_ssd = jnp.zeros((bp, ssd_heads, SSD_HEAD_DIM, SSD_STATE), F32)
        yp, kp, vp, cp, sp = _trunk_layer(yp, p_prompt[i], None, None, zero_conv, zero_ssd, wts, **dims)
        ys, ks, vs, cs, ss = _trunk_layer(ys, p_sample[i], cache_k[i], cache_v[i], state_conv[i], state_ssd[i],
                                          wts, **dims)
        for lst, val in zip(outs, (kp, vp, cp, sp, ks, vs, cs, ss)):
            lst.append(val)
    stacked = [jnp.stack(lst) for lst in outs]
    return (yp, ys, *stacked)
```

```python
import functools
import math

import jax
import jax.numpy as jnp
from jax import lax
from jax.experimental import pallas as pl
from jax.experimental.pallas import tpu as pltpu

F32 = jnp.float32
BF16 = jnp.bfloat16

EPS = 1e-6
SB_HEAD_DIM = 128
SSD_HEAD_DIM = 64
SSD_STATE = 128
SSD_GROUPS = 8
CONV_WIDTH = 4
LANES = 128
SUBLANES = 8
V7X_VMEM_BYTES = 64 * 2**20


def _vmem_limit(block_bytes, scratch_bytes, temp_bytes):
    need = 2 * block_bytes + scratch_bytes + temp_bytes
    return int(min(need, V7X_VMEM_BYTES - 2 * 2**20))


def _nbytes(shape, dtype):
    return math.prod(shape) * jnp.dtype(dtype).itemsize


def _sigmoid(x):
    return 1.0 / (1.0 + jnp.exp(-x))


def _silu(x):
    return x * _sigmoid(x)


def _split_bf16(x, parts):
    out = []
    r = x
    for p in range(parts):
        h = r.astype(BF16)
        out.append(h)
        if p + 1 < parts:
            r = r - h.astype(F32)
    return out


def _dot(a, b):
    return jnp.dot(a, b, preferred_element_type=F32)


def _dot_nt(a, b):
    return lax.dot_general(a, b, (((1,), (1,)), ((), ())), preferred_element_type=F32)


def _dot_split(x, m, parts):
    acc = None
    for h in _split_bf16(x, parts):
        t = _dot(h, m)
        acc = t if acc is None else acc + t
    return acc


def _prenorm_kernel(x_ref, nw_ref, h_ref, *, row_chunk):
    tm = x_ref.shape[0]

    def body(r, carry):
        rows = pl.ds(pl.multiple_of(r * row_chunk, row_chunk), row_chunk)
        xf = x_ref[rows, :]
        ms = jnp.mean(xf * xf, axis=-1, keepdims=True)
        h_ref[rows, :] = (xf * lax.rsqrt(ms + EPS) * nw_ref[...]).astype(BF16)
        return carry

    lax.fori_loop(0, tm // row_chunk, body, 0, unroll=2)


def _prenorm(x, norm_w, *, tm):
    m, d = x.shape
    assert m % tm == 0
    block_bytes = _nbytes((tm, d), F32) + _nbytes((tm, d), BF16)
    return pl.pallas_call(
        functools.partial(_prenorm_kernel, row_chunk=32),
        grid=(m // tm,),
        in_specs=[pl.BlockSpec((tm, d), lambda i: (i, 0)), pl.BlockSpec((1, d), lambda i: (0, 0))],
        out_specs=pl.BlockSpec((tm, d), lambda i: (i, 0)),
        out_shape=jax.ShapeDtypeStruct((m, d), BF16),
        compiler_params=pltpu.CompilerParams(
            dimension_semantics=("arbitrary",),
            vmem_limit_bytes=_vmem_limit(block_bytes, 0, 8 * _nbytes((32, d), F32))),
        name="prenorm",
    )(x, norm_w)


def _in_proj_kernel(h_ref, w_ref, wdt_ref, q_ref, k_ref, v_ref, g_ref, xbc_ref, z_ref, dt_ref, *, bounds, q_scale):
    j = pl.program_id(1)

    def segment(lo, hi, out_ref, scale):
        @pl.when((j >= lo) & (j < hi))
        def _():
            acc = _dot_nt(h_ref[...], w_ref[...])
            if scale is not None:
                acc = acc * scale
            out_ref[...] = acc.astype(out_ref.dtype)

    outs = (q_ref, k_ref, v_ref, g_ref, xbc_ref, z_ref)
    for s, out_ref in enumerate(outs):
        segment(bounds[s], bounds[s + 1], out_ref, q_scale if s == 0 else None)

    @pl.when(j == bounds[-1])
    def _dt():
        dt_ref[...] = _dot_nt(h_ref[...], wdt_ref[...])


def _in_proj(h, w_t, w_dt_t, *, d_sb, conv_ch, d_ssd, tm):
    m, d = h.shape
    tn = math.gcd(math.gcd(d_sb, 512), math.gcd(conv_ch, d_ssd))
    widths = (d_sb, d_sb, d_sb, d_sb, conv_ch, d_ssd)
    dtypes = (BF16, F32, F32, BF16, F32, BF16)
    bounds = [0]
    for w in widths:
        bounds.append(bounds[-1] + w // tn)
    bounds = tuple(bounds)
    n_tiles = bounds[-1]
    assert w_t.shape[1] == d and w_t.shape[0] >= n_tiles * tn and w_dt_t.shape == (LANES, d) and m % tm == 0

    def out_map(lo, hi):
        return lambda i, j: (i, jnp.clip(j - lo, 0, hi - lo - 1))

    out_shapes = [jax.ShapeDtypeStruct((m, wd), dt) for wd, dt in zip(widths, dtypes)]
    out_shapes.append(jax.ShapeDtypeStruct((m, LANES), F32))
    out_specs = [pl.BlockSpec((tm, tn), out_map(bounds[s], bounds[s + 1])) for s in range(len(widths))]
    out_specs.append(pl.BlockSpec((tm, LANES), lambda i, j: (i, 0)))

    block_bytes = (_nbytes((tm, d), BF16) + _nbytes((tn, d), BF16) + _nbytes((LANES, d), BF16)
                   + sum(_nbytes((tm, tn), dt) for dt in dtypes) + _nbytes((tm, LANES), F32))
    temp_bytes = 2 * _nbytes((tm, tn), F32)
    kern = functools.partial(_in_proj_kernel, bounds=bounds, q_scale=SB_HEAD_DIM ** -0.5)
    return pl.pallas_call(
        kern,
        grid=(m // tm, n_tiles + 1),
        in_specs=[
            pl.BlockSpec((tm, d), lambda i, j: (i, 0)),
            pl.BlockSpec((tn, d), lambda i, j: (jnp.minimum(j, n_tiles - 1), 0)),
            pl.BlockSpec((LANES, d), lambda i, j: (0, 0)),
        ],
        out_specs=out_specs,
        out_shape=out_shapes,
        compiler_params=pltpu.CompilerParams(
            dimension_semantics=("arbitrary", "arbitrary"),
            vmem_limit_bytes=_vmem_limit(block_bytes, 0, temp_bytes)),
        name="in_proj",
    )(h, w_t, w_dt_t)


def _attn_kernel(*refs, t, tk, tp, n_past, nh):
    if n_past:
        q_ref, k_ref, v_ref, g_ref, kp_ref, vp_ref, u_ref, o_ref, acc_ref, carry_ref = refs
    else:
        q_ref, k_ref, v_ref, g_ref, u_ref, o_ref, acc_ref, carry_ref = refs
    d = SB_HEAD_DIM
    row = lax.broadcasted_iota(jnp.int32, (tk, tk), 0)
    col = lax.broadcasted_iota(jnp.int32, (tk, tk), 1)
    causal = col < row

    def strip(q, kb, vb, u, r0, n_diag, hs, n):
        z = _dot_nt(q, kb.astype(BF16))
        sp = jnp.log(1.0 + jnp.exp(-jnp.abs(z)))
        log_stop = jnp.minimum(z, 0.0) - sp
        log_stay = log_stop - z
        n_rows = z.shape[0]
        has_rest = n_rows > n_diag
        if n_diag:
            diag_stay = jnp.where(causal, log_stay[:n_diag], 0.0)
            log_stay = jnp.concatenate([diag_stay, log_stay[n_diag:]], axis=0) if has_rest else diag_stay
        later = _dot(log_stay.astype(BF16), u)
        total = jnp.sum(log_stay, axis=-1, keepdims=True)
        parts = []
        if n_diag:
            att_diag = jnp.where(causal, jnp.exp(log_stop[:n_diag] + later[:n_diag]), 0.0)
            parts.append(att_diag)
        if has_rest:
            carry = carry_ref[n, r0 + n_diag:r0 + n_rows, :]
            parts.append(jnp.exp(log_stop[n_diag:] + later[n_diag:] + carry))
        att = jnp.concatenate(parts, axis=0) if len(parts) > 1 else parts[0]
        pv = _dot(att.astype(BF16), vb.astype(BF16))
        if n_diag:
            acc_ref[r0:r0 + n_diag, hs] = pv[:n_diag]
            carry_ref[n, r0:r0 + n_diag, :] = total[:n_diag]
        if has_rest:
            acc_ref[r0 + n_diag:r0 + n_rows, hs] += pv[n_diag:]
            carry_ref[n, r0 + n_diag:r0 + n_rows, :] = carry + total[n_diag:]

    u_new = u_ref[:tk, :tk]
    for c in reversed(range(t // tk)):
        r0 = c * tk
        for n in range(nh):
            hs = slice(n * d, (n + 1) * d)
            strip(q_ref[r0:, hs], k_ref[r0:r0 + tk, hs], v_ref[r0:r0 + tk, hs], u_new, r0, tk, hs, n)
    if n_past:
        u_past = u_ref[:tp, :tp]
        for c in reversed(range(n_past)):
            for n in range(nh):
                hs = slice(n * d, (n + 1) * d)
                strip(q_ref[:, hs], kp_ref[c * tp:(c + 1) * tp, hs], vp_ref[c * tp:(c + 1) * tp, hs], u_past,
                      0, 0, hs, n)
    o_ref[...] = (acc_ref[...] * _silu(g_ref[...].astype(F32))).astype(o_ref.dtype)


def _attention(q, k, v, g, k_past, v_past, *, tk, tp, nh):
    b, t, hd = k.shape
    d = SB_HEAD_DIM
    heads = hd // d
    p = 0 if k_past is None else k_past.shape[1]
    assert t % tk == 0 and p % tp == 0 and heads % nh == 0
    n_past = p // tp
    tu = max(tk, tp) if n_past else tk
    ridx = lax.broadcasted_iota(jnp.int32, (tu, tu), 0)
    cidx = lax.broadcasted_iota(jnp.int32, (tu, tu), 1)
    u = (ridx > cidx).astype(BF16)
    w = nh * d

    head_spec = pl.BlockSpec((None, t, w), lambda bi, h: (bi, 0, h))
    in_specs = [head_spec, head_spec, head_spec, head_spec]
    args = [q, k, v, g]
    block_bytes = 3 * _nbytes((t, w), BF16) + 2 * _nbytes((t, w), F32) + _nbytes((tu, tu), BF16)
    if n_past:
        past_spec = pl.BlockSpec((None, p, w), lambda bi, h: (bi, 0, h))
        in_specs += [past_spec, past_spec]
        args += [k_past, v_past]
        block_bytes += 2 * _nbytes((p, w), F32)
    in_specs.append(pl.BlockSpec((tu, tu), lambda bi, h: (0, 0)))
    args.append(u)
    scratch_bytes = _nbytes((t, w), F32) + nh * _nbytes((t, LANES), F32)
    temp_bytes = 8 * _nbytes((t, tu), F32)
    kern = functools.partial(_attn_kernel, t=t, tk=tk, tp=tp, n_past=n_past, nh=nh)
    return pl.pallas_call(
        kern,
        grid=(b, heads // nh),
        in_specs=in_specs,
        out_specs=head_spec,
        out_shape=jax.ShapeDtypeStruct((b, t, hd), BF16),
        scratch_shapes=[pltpu.VMEM((t, w), F32), pltpu.VMEM((nh, t, 1), F32)],
        compiler_params=pltpu.CompilerParams(
            dimension_semantics=("arbitrary", "arbitrary"),
            vmem_limit_bytes=_vmem_limit(block_bytes, scratch_bytes, temp_bytes)),
        name="stick_breaking",
    )(*args)


def _ssd_kernel(xbc_ref, dt_ref, z_ref, convp_ref, statep_ref, cw_ref, cb_ref, dtb_ref, alog_ref, dskip_ref, e_ref,
                tri_ref, y_ref, state_ref, ext_ref, act_ref, ht_ref, *, chunk, d_ssd, col_slab):
    c = pl.program_id(1)
    n_chunks = pl.num_programs(1)
    ch = xbc_ref.shape[-1]
    gn = SSD_GROUPS * SSD_STATE
    hpg = d_ssd // SSD_HEAD_DIM // SSD_GROUPS
    gw = hpg * SSD_HEAD_DIM
    pad = SUBLANES

    @pl.when(c == 0)
    def _init():
        ext_ref[0:pad, :] = jnp.zeros((pad, ch), F32)
        ext_ref[pad - (CONV_WIDTH - 1):pad, :] = convp_ref[...]
        for g in range(SSD_GROUPS):
            ht_ref[g] = statep_ref[g].T

    @pl.when(c > 0)
    def _shift():
        ext_ref[0:pad, :] = ext_ref[chunk:chunk + pad, :]

    ext_ref[pad:pad + chunk, :] = xbc_ref[...]

    for s in range(ch // col_slab):
        cols = slice(s * col_slab, (s + 1) * col_slab)
        acc = cb_ref[:, cols]
        for jj in range(CONV_WIDTH):
            lo = pad - (CONV_WIDTH - 1) + jj
            acc = acc + cw_ref[jj:jj + 1, cols] * ext_ref[lo:lo + chunk, cols]
        act_ref[:, cols] = _silu(acc)

    dt_in = dt_ref[...] + dtb_ref[...]
    dt = jnp.maximum(dt_in, 0.0) + jnp.log(1.0 + jnp.exp(-jnp.abs(dt_in)))
    da = dt * (-jnp.exp(alog_ref[...]))
    tri = tri_ref[...]
    acum = _dot_tri(tri, da)
    last = acum[chunk - 1:chunk, :]
    acum_t = acum.T
    dt_t = dt.T
    row = lax.broadcasted_iota(jnp.int32, (chunk, chunk), 0)
    col = lax.broadcasted_iota(jnp.int32, (chunk, chunk), 1)
    causal = col <= row

    e = e_ref[...]
    de_e = _dot_split(jnp.exp(last - acum) * dt, e, 2)
    ea_e = _dot_split(jnp.exp(acum), e, 2)
    bd_e = _dot_split(jnp.broadcast_to(jnp.exp(last), (SUBLANES, LANES)), e, 2)[0:1, :]

    lane = lax.broadcasted_iota(jnp.int32, (chunk, gw), 1)
    for g in range(SSD_GROUPS):
        gs = slice(g * gw, (g + 1) * gw)
        xg = act_ref[:, gs]
        bg = act_ref[:, d_ssd + g * SSD_STATE:d_ssd + (g + 1) * SSD_STATE].astype(BF16)
        cg = act_ref[:, d_ssd + gn + g * SSD_STATE:d_ssd + gn + (g + 1) * SSD_STATE].astype(BF16)
        cb = lax.dot_general(cg, bg, (((1,), (1,)), ((), ())), preferred_element_type=F32)
        xgb = xg.astype(BF16)
        y_intra = None
        for hh in range(hpg):
            h = g * hpg + hh
            diff = acum[:, h:h + 1] - acum_t[h:h + 1, :]
            decay = jnp.exp(jnp.where(causal, diff, -jnp.inf))
            w = (cb * decay * dt_t[h:h + 1, :]).astype(BF16)
            in_head = (lane >= hh * SSD_HEAD_DIM) & (lane < (hh + 1) * SSD_HEAD_DIM)
            t = _dot(w, jnp.where(in_head, xgb, jnp.zeros_like(xgb)))
            y_intra = t if y_intra is None else y_intra + t
        h_prev = ht_ref[g]
        y_inter = _dot(cg, h_prev.astype(BF16)) * ea_e[:, gs]
        xw = (xg * de_e[:, gs]).astype(BF16)
        s_t = lax.dot_general(bg, xw, (((0,), (0,)), ((), ())), preferred_element_type=F32)
        ht_ref[g] = h_prev * bd_e[:, gs] + s_t
        y = y_intra + y_inter + dskip_ref[:, gs] * xg
        y_ref[:, gs] = (y * _silu(z_ref[:, gs].astype(F32))).astype(y_ref.dtype)

    @pl.when(c == n_chunks - 1)
    def _final():
        for g in range(SSD_GROUPS):
            state_ref[g] = ht_ref[g].T


def _dot_tri(tri, x):
    acc = None
    for h in _split_bf16(x, 3):
        t = _dot(tri, h)
        acc = t if acc is None else acc + t
    return acc


def _ssd(xbc, dt_raw, z, conv_past, state_past, conv_w, conv_b, dt_bias, a_log, d_skip, *, d_ssd, chunk):
    b, t, ch = xbc.shape
    heads = d_ssd // SSD_HEAD_DIM
    hpg = heads // SSD_GROUPS
    gw = hpg * SSD_HEAD_DIM
    assert t % chunk == 0 and heads <= LANES
    n_chunks = t // chunk
    pad_h = LANES - heads
    dtb = jnp.pad(dt_bias.astype(F32), (0, pad_h)).reshape(1, LANES)
    alog = jnp.pad(a_log.astype(F32), (0, pad_h)).reshape(1, LANES)
    dskip = jnp.repeat(d_skip.astype(F32), SSD_HEAD_DIM).reshape(1, d_ssd)
    hidx = lax.broadcasted_iota(jnp.int32, (LANES, d_ssd), 0)
    lidx = lax.broadcasted_iota(jnp.int32, (LANES, d_ssd), 1)
    expand = (lidx // SSD_HEAD_DIM == hidx).astype(BF16)
    ridx = lax.broadcasted_iota(jnp.int32, (chunk, chunk), 0)
    cidx = lax.broadcasted_iota(jnp.int32, (chunk, chunk), 1)
    tri = (cidx <= ridx).astype(BF16)
    state_g = state_past.reshape(b, SSD_GROUPS, gw, SSD_STATE)

    full = lambda shape: pl.BlockSpec(shape, lambda bi, c: (0,) * len(shape))
    col_slab = 512
    block_bytes = (_nbytes((chunk, ch), F32) + _nbytes((chunk, LANES), F32) + _nbytes((SUBLANES, ch), F32)
                   + 2 * _nbytes((SSD_GROUPS, gw, SSD_STATE), F32) + _nbytes((CONV_WIDTH + 1, ch), F32) * 2
                   + _nbytes((LANES, d_ssd), BF16) + _nbytes((chunk, chunk), BF16)
                   + 2 * _nbytes((chunk, d_ssd), BF16))
    scratch_bytes = (_nbytes((chunk + SUBLANES, ch), F32) + _nbytes((chunk, ch), F32)
                     + _nbytes((SSD_GROUPS, SSD_STATE, gw), F32))
    temp_bytes = 4 * _nbytes((chunk, d_ssd), F32) + 16 * _nbytes((chunk, max(chunk, LANES)), F32)
    kern = functools.partial(_ssd_kernel, chunk=chunk, d_ssd=d_ssd, col_slab=col_slab)
    y, state = pl.pallas_call(
        kern,
        grid=(b, n_chunks),
        in_specs=[
            pl.BlockSpec((None, chunk, ch), lambda bi, c: (bi, c, 0)),
            pl.BlockSpec((None, chunk, LANES), lambda bi, c: (bi, c, 0)),
            pl.BlockSpec((None, chunk, d_ssd), lambda bi, c: (bi, c, 0)),
            pl.BlockSpec((None, CONV_WIDTH - 1, ch), lambda bi, c: (bi, 0, 0)),
            pl.BlockSpec((None, SSD_GROUPS, gw, SSD_STATE), lambda bi, c: (bi, 0, 0, 0)),
            full((CONV_WIDTH, ch)), full((1, ch)), full((1, LANES)), full((1, LANES)), full((1, d_ssd)),
            full((LANES, d_ssd)), full((chunk, chunk)),
        ],
        out_specs=[
            pl.BlockSpec((None, chunk, d_ssd), lambda bi, c: (bi, c, 0)),
            pl.BlockSpec((None, SSD_GROUPS, gw, SSD_STATE), lambda bi, c: (bi, 0, 0, 0)),
        ],
        out_shape=[jax.ShapeDtypeStruct((b, t, d_ssd), BF16),
                   jax.ShapeDtypeStruct((b, SSD_GROUPS, gw, SSD_STATE), F32)],
        scratch_shapes=[pltpu.VMEM((chunk + SUBLANES, ch), F32), pltpu.VMEM((chunk, ch), F32),
                        pltpu.VMEM((SSD_GROUPS, SSD_STATE, gw), F32)],
        compiler_params=pltpu.CompilerParams(
            dimension_semantics=("arbitrary", "arbitrary"),
            vmem_limit_bytes=_vmem_limit(block_bytes, scratch_bytes, temp_bytes)),
        name="conv_ssd",
    )(xbc, dt_raw, z, conv_past, state_g, conv_w, conv_b.reshape(1, ch), dtb, alog, dskip, expand, tri)
    return y, state.reshape(b, heads, SSD_HEAD_DIM, SSD_STATE)


def _lagged_maps(n_rows, n_col):
    cur = lambda i, j: (jnp.minimum(i, n_rows - 1), 0)
    wcol = lambda i, j: (0, jnp.where(i < n_rows, j, n_col - 1))
    prev = lambda i, j: (jnp.maximum(i - 1, 0), jnp.where(i > 0, j, 0))
    ncol = lambda i, j: (0, jnp.where(i > 0, j, 0))
    return cur, wcol, prev, ncol


def _lagged_steps(i, n_rows, finish, matmul):
    if n_rows > 1:
        @pl.when((i > 0) & (i < n_rows))
        def _():
            finish()
            matmul()

    @pl.when(i == 0)
    def _():
        matmul()

    @pl.when(i == n_rows)
    def _():
        finish()


def _out_proj_kernel(usb_ref, ussd_ref, w_ref, x_ref, nw_ref, x1_ref, x1b_ref, m_ref, ssq_ref, inv_ref, ginv_ref,
                     *, n_rows, d_model, row_chunk):
    i = pl.program_id(0)
    j = pl.program_id(1)
    tm, d_sb = usb_ref.shape

    @pl.when(j == 0)
    def _():
        @pl.when(i > 0)
        def _():
            inv_ref[...] = lax.rsqrt(ssq_ref[...] * (1.0 / d_model) + EPS)

        ssq_ref[...] = jnp.zeros_like(ssq_ref)

        @pl.when(i < n_rows)
        def _():
            def body(r, carry):
                rows = pl.ds(pl.multiple_of(r * row_chunk, row_chunk), row_chunk)
                for g, u_ref in enumerate((usb_ref, ussd_ref)):
                    u = u_ref[rows, :].astype(F32)
                    ginv_ref[g, rows, :] = lax.rsqrt(jnp.mean(u * u, axis=-1, keepdims=True) + EPS)
                return carry

            lax.fori_loop(0, tm // row_chunk, body, 0, unroll=2)

    def finish():
        x1 = x_ref[...] + m_ref[j] * inv_ref[...] * nw_ref[...]
        x1_ref[...] = x1
        x1b_ref[...] = x1.astype(BF16)

    def matmul():
        mt = (_dot(usb_ref[...], w_ref[0:d_sb, :]) * ginv_ref[0]
              + _dot(ussd_ref[...], w_ref[d_sb:, :]) * ginv_ref[1])
        m_ref[j] = mt
        ssq_ref[...] += jnp.sum(mt * mt, axis=-1, keepdims=True)

    _lagged_steps(i, n_rows, finish, matmul)


def _out_proj(u_sb, u_ssd, w, x, norm_w, *, tm, tn):
    m, d_sb = u_sb.shape
    d_ssd = u_ssd.shape[1]
    d_mix, d = w.shape
    assert m % tm == 0 and d % tn == 0 and d_mix == d_sb + d_ssd
    n_col = d // tn
    n_rows = m // tm
    cur, wcol, prev, ncol = _lagged_maps(n_rows, n_col)
    block_bytes = (_nbytes((tm, d_mix), BF16) + _nbytes((d_mix, tn), BF16) + 2 * _nbytes((tm, tn), F32)
                   + _nbytes((tm, tn), BF16) + _nbytes((SUBLANES, tn), F32))
    scratch_bytes = _nbytes((tm, d), F32) + 4 * _nbytes((tm, LANES), F32)
    kern = functools.partial(_out_proj_kernel, n_rows=n_rows, d_model=d, row_chunk=32)
    return pl.pallas_call(
        kern,
        grid=(n_rows + 1, n_col),
        in_specs=[
            pl.BlockSpec((tm, d_sb), cur),
            pl.BlockSpec((tm, d_ssd), cur),
            pl.BlockSpec((d_mix, tn), wcol),
            pl.BlockSpec((tm, tn), prev),
            pl.BlockSpec((1, tn), ncol),
        ],
        out_specs=[pl.BlockSpec((tm, tn), prev), pl.BlockSpec((tm, tn), prev)],
        out_shape=[jax.ShapeDtypeStruct((m, d), F32), jax.ShapeDtypeStruct((m, d), BF16)],
        scratch_shapes=[pltpu.VMEM((n_col, tm, tn), F32), pltpu.VMEM((tm, 1), F32), pltpu.VMEM((tm, 1), F32),
                        pltpu.VMEM((2, tm, 1), F32)],
        compiler_params=pltpu.CompilerParams(
            dimension_semantics=("arbitrary", "arbitrary"),
            vmem_limit_bytes=_vmem_limit(block_bytes, scratch_bytes, 3 * _nbytes((tm, tn), F32))),
        name="out_proj",
    )(u_sb, u_ssd, w, x, norm_w.reshape(1, d))


def _ple_kernel(x1b_ref, wg_ref, p_ref, wp_ref, x1_ref, nw_ref, y_ref, ge_ref, ssq_ref, inv_ref, *, n_rows, d_model):
    i = pl.program_id(0)
    j = pl.program_id(1)

    @pl.when(j == 0)
    def _():
        @pl.when(i > 0)
        def _():
            inv_ref[...] = lax.rsqrt(ssq_ref[...] * (1.0 / d_model) + EPS)

        ssq_ref[...] = jnp.zeros_like(ssq_ref)

    def finish():
        y_ref[...] = x1_ref[...] + ge_ref[j] * inv_ref[...] * nw_ref[...]

    def matmul():
        gate = _sigmoid(_dot(x1b_ref[...], wg_ref[...]))
        emb = _dot(p_ref[...].astype(BF16), wp_ref[...])
        ge = gate * emb
        ge_ref[j] = ge
        ssq_ref[...] += jnp.sum(ge * ge, axis=-1, keepdims=True)

    _lagged_steps(i, n_rows, finish, matmul)


def _ple(x1b, w_gate, p, w_proj, x1, norm_w, *, tm, tn):
    m, d = x1.shape
    dp = p.shape[1]
    assert m % tm == 0 and d % tn == 0
    n_col = d // tn
    n_rows = m // tm
    cur, wcol, prev, ncol = _lagged_maps(n_rows, n_col)
    block_bytes = (_nbytes((tm, d), BF16) + _nbytes((d, tn), BF16) + _nbytes((tm, dp), F32) + _nbytes((dp, tn), BF16)
                   + 2 * _nbytes((tm, tn), F32) + _nbytes((SUBLANES, tn), F32))
    scratch_bytes = _nbytes((tm, d), F32) + 2 * _nbytes((tm, LANES), F32)
    kern = functools.partial(_ple_kernel, n_rows=n_rows, d_model=d)
    return pl.pallas_call(
        kern,
        grid=(n_rows + 1, n_col),
        in_specs=[
            pl.BlockSpec((tm, d), cur),
            pl.BlockSpec((d, tn), wcol),
            pl.BlockSpec((tm, dp), cur),
            pl.BlockSpec((dp, tn), wcol),
            pl.BlockSpec((tm, tn), prev),
            pl.BlockSpec((1, tn), ncol),
        ],
        out_specs=pl.BlockSpec((tm, tn), prev),
        out_shape=jax.ShapeDtypeStruct((m, d), F32),
        scratch_shapes=[pltpu.VMEM((n_col, tm, tn), F32), pltpu.VMEM((tm, 1), F32), pltpu.VMEM((tm, 1), F32)],
        compiler_params=pltpu.CompilerParams(
            dimension_semantics=("arbitrary", "arbitrary"),
            vmem_limit_bytes=_vmem_limit(block_bytes, scratch_bytes, 3 * _nbytes((tm, tn), F32))),
        name="ple",
    )(x1b, w_gate, p, w_proj, x1, norm_w.reshape(1, d))


def _row_tile(m, target):
    return target if m % target == 0 else m


def _trunk_layer(x, p, k_past, v_past, conv_past, ssd_past, wts, *, d_sb, d_ssd, conv_ch):
    (norm_pre, norm_post, w_in_t, w_dt_t, conv_w, conv_b, dt_bias, a_log, d_skip, w_out, w_gate, w_proj,
     ple_norm) = wts
    b, t, d = x.shape
    m = b * t
    xf = x.reshape(m, d)
    tm = _row_tile(m, 1024)
    h = _prenorm(xf, norm_pre.reshape(1, d), tm=_row_tile(m, 256))
    q, k, v, g, xbc, z, dt_raw = _in_proj(h, w_in_t, w_dt_t, d_sb=d_sb, conv_ch=conv_ch, d_ssd=d_ssd, tm=tm)
    tk = 256 if t % 256 == 0 else t
    kp = None if k_past is None else k_past.reshape(b, k_past.shape[1], d_sb)
    vp = None if v_past is None else v_past.reshape(b, v_past.shape[1], d_sb)
    heads = d_sb // SB_HEAD_DIM
    nh = 1 if t > 256 or heads % 4 else 4
    u_sb = _attention(q.reshape(b, t, d_sb), k.reshape(b, t, d_sb), v.reshape(b, t, d_sb), g.reshape(b, t, d_sb),
                      kp, vp, tk=tk, tp=256, nh=nh)
    chunk = 128 if t % 128 == 0 else t
    u_ssd, ssd_new = _ssd(xbc.reshape(b, t, conv_ch), dt_raw.reshape(b, t, LANES), z.reshape(b, t, d_ssd),
                          conv_past, ssd_past, conv_w, conv_b, dt_bias, a_log, d_skip, d_ssd=d_ssd, chunk=chunk)
    x1, x1b = _out_proj(u_sb.reshape(m, d_sb), u_ssd.reshape(m, d_ssd), w_out, xf, norm_post,
                        tm=tm, tn=512 if d % 512 == 0 else d)
    yo = _ple(x1b, w_gate, p.reshape(m, p.shape[-1]), w_proj, x1, ple_norm, tm=tm, tn=512 if d % 512 == 0 else d)
    conv_new = xbc.reshape(b, t, conv_ch)[:, t - (CONV_WIDTH - 1):, :]
    return (yo.reshape(b, t, d), k.reshape(b, t, heads, SB_HEAD_DIM), v.reshape(b, t, heads, SB_HEAD_DIM),
            conv_new, ssd_new)


def kernel(x_prompt, x_sample, cache_k, cache_v, state_conv, state_ssd, p_prompt, p_sample, norm_pre, norm_post,
           w_in, conv_w, conv_b, dt_bias, a_log, d_skip, sb_norm, ssd_norm, w_out, w_ple_gate, w_ple_proj, ple_norm):
    depth = norm_pre.shape[0]
    ssd_heads = dt_bias.shape[1]
    d_ssd = ssd_heads * SSD_HEAD_DIM
    d_sb = sb_norm.shape[1]
    conv_ch = conv_w.shape[2]
    n_main = 4 * d_sb + conv_ch + d_ssd
    bp = x_prompt.shape[0]
    yp, ys = x_prompt, x_sample
    outs = [[] for _ in range(8)]
    for i in range(depth):
        w_in_t = jnp.swapaxes(w_in[i], 0, 1).astype(BF16)
        w_dt_t = jnp.pad(w_in_t[n_main:], ((0, LANES - ssd_heads), (0, 0)))
        group_gain = jnp.concatenate([sb_norm[i], ssd_norm[i]]).astype(F32)
        w_out_g = (group_gain[:, None] * w_out[i]).astype(BF16)
        wts = (norm_pre[i], norm_post[i], w_in_t, w_dt_t, conv_w[i], conv_b[i], dt_bias[i], a_log[i], d_skip[i],
               w_out_g, w_ple_gate[i].astype(BF16), w_ple_proj[i].astype(BF16), ple_norm[i])
        dims = dict(d_sb=d_sb, d_ssd=d_ssd, conv_ch=conv_ch)
        zero_conv = jnp.zeros((bp, CONV_WIDTH - 1, conv_ch), F32)
        zero_ssd = jnp.zeros((bp, ssd_heads, SSD_HEAD_DIM, SSD_STATE), F32)
        yp, kp, vp, cp, sp = _trunk_layer(yp, p_prompt[i], None, None, zero_conv, zero_ssd, wts, **dims)
        ys, ks, vs, cs, ss = _trunk_layer(ys, p_sample[i], cache_k[i], cache_v[i], state_conv[i], state_ssd[i],
                                          wts, **dims)
        for lst, val in zip(outs, (kp, vp, cp, sp, ks, vs, cs, ss)):
            lst.append(val)
    stacked = [jnp.stack(lst) for lst in outs]
    return (yp, ys, *stacked)
```

```python
import functools
import math

import jax
import jax.numpy as jnp
from jax import lax
from jax.experimental import pallas as pl
from jax.experimental.pallas import tpu as pltpu

F32 = jnp.float32
BF16 = jnp.bfloat16

EPS = 1e-6
SB_HEAD_DIM = 128
SSD_HEAD_DIM = 64
SSD_STATE = 128
SSD_GROUPS = 8
CONV_WIDTH = 4
LANES = 128
SUBLANES = 8
V7X_VMEM_BYTES = 64 * 2**20


def _vmem_limit(block_bytes, scratch_bytes, temp_bytes):
    need = 2 * block_bytes + scratch_bytes + temp_bytes
    return int(min(need, V7X_VMEM_BYTES - 2 * 2**20))


def _nbytes(shape, dtype):
    return math.prod(shape) * jnp.dtype(dtype).itemsize


def _sigmoid(x):
    return 1.0 / (1.0 + jnp.exp(-x))


def _silu(x):
    return x * _sigmoid(x)


def _split_bf16(x, parts):
    out = []
    r = x
    for p in range(parts):
        h = r.astype(BF16)
        out.append(h)
        if p + 1 < parts:
            r = r - h.astype(F32)
    return out


def _dot(a, b):
    return jnp.dot(a, b, preferred_element_type=F32)


def _dot_nt(a, b):
    return lax.dot_general(a, b, (((1,), (1,)), ((), ())), preferred_element_type=F32)


def _dot_split(x, m, parts):
    acc = None
    for h in _split_bf16(x, parts):
        t = _dot(h, m)
        acc = t if acc is None else acc + t
    return acc


def _prenorm_kernel(x_ref, nw_ref, h_ref, *, row_chunk):
    tm = x_ref.shape[0]

    def body(r, carry):
        rows = pl.ds(pl.multiple_of(r * row_chunk, row_chunk), row_chunk)
        xf = x_ref[rows, :]
        ms = jnp.mean(xf * xf, axis=-1, keepdims=True)
        h_ref[rows, :] = (xf * lax.rsqrt(ms + EPS) * nw_ref[...]).astype(BF16)
        return carry

    lax.fori_loop(0, tm // row_chunk, body, 0, unroll=2)


def _prenorm(x, norm_w, *, tm):
    m, d = x.shape
    assert m % tm == 0
    block_bytes = _nbytes((tm, d), F32) + _nbytes((tm, d), BF16)
    return pl.pallas_call(
        functools.partial(_prenorm_kernel, row_chunk=32),
        grid=(m // tm,),
        in_specs=[pl.BlockSpec((tm, d), lambda i: (i, 0)), pl.BlockSpec((1, d), lambda i: (0, 0))],
        out_specs=pl.BlockSpec((tm, d), lambda i: (i, 0)),
        out_shape=jax.ShapeDtypeStruct((m, d), BF16),
        compiler_params=pltpu.CompilerParams(
            dimension_semantics=("arbitrary",),
            vmem_limit_bytes=_vmem_limit(block_bytes, 0, 8 * _nbytes((32, d), F32))),
        name="prenorm",
    )(x, norm_w)


def _in_proj_kernel(h_ref, w_ref, wdt_ref, q_ref, k_ref, v_ref, g_ref, xbc_ref, z_ref, dt_ref, *, bounds, q_scale):
    j = pl.program_id(1)

    def segment(lo, hi, out_ref, scale):
        @pl.when((j >= lo) & (j < hi))
        def _():
            acc = _dot_nt(h_ref[...], w_ref[...])
            if scale is not None:
                acc = acc * scale
            out_ref[...] = acc.astype(out_ref.dtype)

    outs = (q_ref, k_ref, v_ref, g_ref, xbc_ref, z_ref)
    for s, out_ref in enumerate(outs):
        segment(bounds[s], bounds[s + 1], out_ref, q_scale if s == 0 else None)

    @pl.when(j == bounds[-1])
    def _dt():
        dt_ref[...] = _dot_nt(h_ref[...], wdt_ref[...])


def _in_proj(h, w_t, w_dt_t, *, d_sb, conv_ch, d_ssd, tm):
    m, d = h.shape
    tn = math.gcd(math.gcd(d_sb, 512), math.gcd(conv_ch, d_ssd))
    widths = (d_sb, d_sb, d_sb, d_sb, conv_ch, d_ssd)
    dtypes = (BF16, F32, F32, BF16, F32, BF16)
    bounds = [0]
    for w in widths:
        bounds.append(bounds[-1] + w // tn)
    bounds = tuple(bounds)
    n_tiles = bounds[-1]
    assert w_t.shape[1] == d and w_t.shape[0] >= n_tiles * tn and w_dt_t.shape == (LANES, d) and m % tm == 0

    def out_map(lo, hi):
        return lambda i, j: (i, jnp.clip(j - lo, 0, hi - lo - 1))

    out_shapes = [jax.ShapeDtypeStruct((m, wd), dt) for wd, dt in zip(widths, dtypes)]
    out_shapes.append(jax.ShapeDtypeStruct((m, LANES), F32))
    out_specs = [pl.BlockSpec((tm, tn), out_map(bounds[s], bounds[s + 1])) for s in range(len(widths))]
    out_specs.append(pl.BlockSpec((tm, LANES), lambda i, j: (i, 0)))

    block_bytes = (_nbytes((tm, d), BF16) + _nbytes((tn, d), BF16) + _nbytes((LANES, d), BF16)
                   + sum(_nbytes((tm, tn), dt) for dt in dtypes) + _nbytes((tm, LANES), F32))
    temp_bytes = 2 * _nbytes((tm, tn), F32)
    kern = functools.partial(_in_proj_kernel, bounds=bounds, q_scale=SB_HEAD_DIM ** -0.5)
    return pl.pallas_call(
        kern,
        grid=(m // tm, n_tiles + 1),
        in_specs=[
            pl.BlockSpec((tm, d), lambda i, j: (i, 0)),
            pl.BlockSpec((tn, d), lambda i, j: (jnp.minimum(j, n_tiles - 1), 0)),
            pl.BlockSpec((LANES, d), lambda i, j: (0, 0)),
        ],
        out_specs=out_specs,
        out_shape=out_shapes,
        compiler_params=pltpu.CompilerParams(
            dimension_semantics=("arbitrary", "arbitrary"),
            vmem_limit_bytes=_vmem_limit(block_bytes, 0, temp_bytes)),
        name="in_proj",
    )(h, w_t, w_dt_t)


def _attn_kernel(*refs, t, tk, tp, n_past, nh):
    if n_past:
        q_ref, k_ref, v_ref, g_ref, kp_ref, vp_ref, u_ref, o_ref, acc_ref, carry_ref = refs
    else:
        q_ref, k_ref, v_ref, g_ref, u_ref, o_ref, acc_ref, carry_ref = refs
    d = SB_HEAD_DIM
    row = lax.broadcasted_iota(jnp.int32, (tk, tk), 0)
    col = lax.broadcasted_iota(jnp.int32, (tk, tk), 1)
    causal = col < row

    def strip(q, kb, vb, u, r0, n_diag, hs, n):
        z = _dot_nt(q, kb.astype(BF16))
        sp = jnp.log(1.0 + jnp.exp(-jnp.abs(z)))
        log_stop = jnp.minimum(z, 0.0) - sp
        log_stay = log_stop - z
        n_rows = z.shape[0]
        has_rest = n_rows > n_diag
        if n_diag:
            diag_stay = jnp.where(causal, log_stay[:n_diag], 0.0)
            log_stay = jnp.concatenate([diag_stay, log_stay[n_diag:]], axis=0) if has_rest else diag_stay
        later = _dot(log_stay.astype(BF16), u)
        total = jnp.sum(log_stay, axis=-1, keepdims=True)
        parts = []
        if n_diag:
            att_diag = jnp.where(causal, jnp.exp(log_stop[:n_diag] + later[:n_diag]), 0.0)
            parts.append(att_diag)
        if has_rest:
            carry = carry_ref[n, r0 + n_diag:r0 + n_rows, :]
            parts.append(jnp.exp(log_stop[n_diag:] + later[n_diag:] + carry))
        att = jnp.concatenate(parts, axis=0) if len(parts) > 1 else parts[0]
        pv = _dot(att.astype(BF16), vb.astype(BF16))
        if n_diag:
            acc_ref[r0:r0 + n_diag, hs] = pv[:n_diag]
            carry_ref[n, r0:r0 + n_diag, :] = total[:n_diag]
        if has_rest:
            acc_ref[r0 + n_diag:r0 + n_rows, hs] += pv[n_diag:]
            carry_ref[n, r0 + n_diag:r0 + n_rows, :] = carry + total[n_diag:]

    u_new = u_ref[:tk, :tk]
    for c in reversed(range(t // tk)):
        r0 = c * tk
        for n in range(nh):
            hs = slice(n * d, (n + 1) * d)
            strip(q_ref[r0:, hs], k_ref[r0:r0 + tk, hs], v_ref[r0:r0 + tk, hs], u_new, r0, tk, hs, n)
    if n_past:
        u_past = u_ref[:tp, :tp]
        for c in reversed(range(n_past)):
            kt = jnp.swapaxes(kp_ref[c * tp:(c + 1) * tp], 0, 1)
            vt = jnp.swapaxes(vp_ref[c * tp:(c + 1) * tp], 0, 1)
            for n in range(nh):
                hs = slice(n * d, (n + 1) * d)
                strip(q_ref[:, hs], kt[n], vt[n], u_past, 0, 0, hs, n)
    o_ref[...] = (acc_ref[...] * _silu(g_ref[...].astype(F32))).astype(o_ref.dtype)


def _attention(q, k, v, g, k_past, v_past, *, tk, tp, nh):
    b, t, hd = k.shape
    d = SB_HEAD_DIM
    heads = hd // d
    p = 0 if k_past is None else k_past.shape[1]
    assert t % tk == 0 and p % tp == 0 and heads % nh == 0 and (p == 0 or nh == heads)
    n_past = p // tp
    tu = max(tk, tp) if n_past else tk
    ridx = lax.broadcasted_iota(jnp.int32, (tu, tu), 0)
    cidx = lax.broadcasted_iota(jnp.int32, (tu, tu), 1)
    u = (ridx > cidx).astype(BF16)
    w = nh * d

    head_spec = pl.BlockSpec((None, t, w), lambda bi, h: (bi, 0, h))
    in_specs = [head_spec, head_spec, head_spec, head_spec]
    args = [q, k, v, g]
    block_bytes = 3 * _nbytes((t, w), BF16) + 2 * _nbytes((t, w), F32) + _nbytes((tu, tu), BF16)
    if n_past:
        past_spec = pl.BlockSpec((None, p, heads, d), lambda bi, h: (bi, 0, 0, 0))
        in_specs += [past_spec, past_spec]
        args += [k_past, v_past]
        block_bytes += 2 * _nbytes((p, w), F32)
    in_specs.append(pl.BlockSpec((tu, tu), lambda bi, h: (0, 0)))
    args.append(u)
    scratch_bytes = _nbytes((t, w), F32) + nh * _nbytes((t, LANES), F32)
    temp_bytes = 8 * _nbytes((t, tu), F32)
    kern = functools.partial(_attn_kernel, t=t, tk=tk, tp=tp, n_past=n_past, nh=nh)
    return pl.pallas_call(
        kern,
        grid=(b, heads // nh),
        in_specs=in_specs,
        out_specs=head_spec,
        out_shape=jax.ShapeDtypeStruct((b, t, hd), BF16),
        scratch_shapes=[pltpu.VMEM((t, w), F32), pltpu.VMEM((nh, t, 1), F32)],
        compiler_params=pltpu.CompilerParams(
            dimension_semantics=("arbitrary", "arbitrary"),
            vmem_limit_bytes=_vmem_limit(block_bytes, scratch_bytes, temp_bytes)),
        name="stick_breaking",
    )(*args)


def _ssd_kernel(xbc_ref, dt_ref, z_ref, convp_ref, statep_ref, cw_ref, cb_ref, dtb_ref, alog_ref, dskip_ref, e_ref,
                tri_ref, y_ref, state_ref, ext_ref, act_ref, ht_ref, *, chunk, d_ssd, col_slab):
    c = pl.program_id(1)
    n_chunks = pl.num_programs(1)
    ch = xbc_ref.shape[-1]
    gn = SSD_GROUPS * SSD_STATE
    hpg = d_ssd // SSD_HEAD_DIM // SSD_GROUPS
    gw = hpg * SSD_HEAD_DIM
    pad = SUBLANES

    @pl.when(c == 0)
    def _init():
        ext_ref[0:pad, :] = jnp.zeros((pad, ch), F32)
        ext_ref[pad - (CONV_WIDTH - 1):pad, :] = convp_ref[...]
        for g in range(SSD_GROUPS):
            ht_ref[g] = statep_ref[g].T

    @pl.when(c > 0)
    def _shift():
        ext_ref[0:pad, :] = ext_ref[chunk:chunk + pad, :]

    ext_ref[pad:pad + chunk, :] = xbc_ref[...]

    for s in range(ch // col_slab):
        cols = slice(s * col_slab, (s + 1) * col_slab)
        acc = cb_ref[:, cols]
        for jj in range(CONV_WIDTH):
            lo = pad - (CONV_WIDTH - 1) + jj
            acc = acc + cw_ref[jj:jj + 1, cols] * ext_ref[lo:lo + chunk, cols]
        act_ref[:, cols] = _silu(acc)

    dt_in = dt_ref[...] + dtb_ref[...]
    dt = jnp.maximum(dt_in, 0.0) + jnp.log(1.0 + jnp.exp(-jnp.abs(dt_in)))
    da = dt * (-jnp.exp(alog_ref[...]))
    tri = tri_ref[...]
    acum = _dot_tri(tri, da)
    last = acum[chunk - 1:chunk, :]
    acum_t = acum.T
    dt_t = dt.T
    row = lax.broadcasted_iota(jnp.int32, (chunk, chunk), 0)
    col = lax.broadcasted_iota(jnp.int32, (chunk, chunk), 1)
    causal = col <= row

    e = e_ref[...]
    de_e = _dot_split(jnp.exp(last - acum) * dt, e, 2)
    ea_e = _dot_split(jnp.exp(acum), e, 2)
    bd_e = _dot_split(jnp.broadcast_to(jnp.exp(last), (SUBLANES, LANES)), e, 2)[0:1, :]

    lane = lax.broadcasted_iota(jnp.int32, (chunk, gw), 1)
    for g in range(SSD_GROUPS):
        gs = slice(g * gw, (g + 1) * gw)
        xg = act_ref[:, gs]
        bg = act_ref[:, d_ssd + g * SSD_STATE:d_ssd + (g + 1) * SSD_STATE].astype(BF16)
        cg = act_ref[:, d_ssd + gn + g * SSD_STATE:d_ssd + gn + (g + 1) * SSD_STATE].astype(BF16)
        cb = lax.dot_general(cg, bg, (((1,), (1,)), ((), ())), preferred_element_type=F32)
        xgb = xg.astype(BF16)
        y_intra = None
        for hh in range(hpg):
            h = g * hpg + hh
            diff = acum[:, h:h + 1] - acum_t[h:h + 1, :]
            decay = jnp.exp(jnp.where(causal, diff, -jnp.inf))
            w = (cb * decay * dt_t[h:h + 1, :]).astype(BF16)
            in_head = (lane >= hh * SSD_HEAD_DIM) & (lane < (hh + 1) * SSD_HEAD_DIM)
            t = _dot(w, jnp.where(in_head, xgb, jnp.zeros_like(xgb)))
            y_intra = t if y_intra is None else y_intra + t
        h_prev = ht_ref[g]
        y_inter = _dot(cg, h_prev.astype(BF16)) * ea_e[:, gs]
        xw = (xg * de_e[:, gs]).astype(BF16)
        s_t = lax.dot_general(bg, xw, (((0,), (0,)), ((), ())), preferred_element_type=F32)
        ht_ref[g] = h_prev * bd_e[:, gs] + s_t
        y = y_intra + y_inter + dskip_ref[:, gs] * xg
        y_ref[:, gs] = (y * _silu(z_ref[:, gs].astype(F32))).astype(y_ref.dtype)

    @pl.when(c == n_chunks - 1)
    def _final():
        for g in range(SSD_GROUPS):
            state_ref[g] = ht_ref[g].T


def _dot_tri(tri, x):
    acc = None
    for h in _split_bf16(x, 3):
        t = _dot(tri, h)
        acc = t if acc is None else acc + t
    return acc


def _ssd(xbc, dt_raw, z, conv_past, state_past, conv_w, conv_b, dt_bias, a_log, d_skip, *, d_ssd, chunk):
    b, t, ch = xbc.shape
    heads = d_ssd // SSD_HEAD_DIM
    hpg = heads // SSD_GROUPS
    gw = hpg * SSD_HEAD_DIM
    assert t % chunk == 0 and heads <= LANES
    n_chunks = t // chunk
    pad_h = LANES - heads
    dtb = jnp.pad(dt_bias.astype(F32), (0, pad_h)).reshape(1, LANES)
    alog = jnp.pad(a_log.astype(F32), (0, pad_h)).reshape(1, LANES)
    dskip = jnp.repeat(d_skip.astype(F32), SSD_HEAD_DIM).reshape(1, d_ssd)
    hidx = lax.broadcasted_iota(jnp.int32, (LANES, d_ssd), 0)
    lidx = lax.broadcasted_iota(jnp.int32, (LANES, d_ssd), 1)
    expand = (lidx // SSD_HEAD_DIM == hidx).astype(BF16)
    ridx = lax.broadcasted_iota(jnp.int32, (chunk, chunk), 0)
    cidx = lax.broadcasted_iota(jnp.int32, (chunk, chunk), 1)
    tri = (cidx <= ridx).astype(BF16)
    state_g = state_past.reshape(b, SSD_GROUPS, gw, SSD_STATE)

    full = lambda shape: pl.BlockSpec(shape, lambda bi, c: (0,) * len(shape))
    col_slab = 512
    block_bytes = (_nbytes((chunk, ch), F32) + _nbytes((chunk, LANES), F32) + _nbytes((SUBLANES, ch), F32)
                   + 2 * _nbytes((SSD_GROUPS, gw, SSD_STATE), F32) + _nbytes((CONV_WIDTH + 1, ch), F32) * 2
                   + _nbytes((LANES, d_ssd), BF16) + _nbytes((chunk, chunk), BF16)
                   + 2 * _nbytes((chunk, d_ssd), BF16))
    scratch_bytes = (_nbytes((chunk + SUBLANES, ch), F32) + _nbytes((chunk, ch), F32)
                     + _nbytes((SSD_GROUPS, SSD_STATE, gw), F32))
    temp_bytes = 4 * _nbytes((chunk, d_ssd), F32) + 16 * _nbytes((chunk, max(chunk, LANES)), F32)
    kern = functools.partial(_ssd_kernel, chunk=chunk, d_ssd=d_ssd, col_slab=col_slab)
    y, state = pl.pallas_call(
        kern,
        grid=(b, n_chunks),
        in_specs=[
            pl.BlockSpec((None, chunk, ch), lambda bi, c: (bi, c, 0)),
            pl.BlockSpec((None, chunk, LANES), lambda bi, c: (bi, c, 0)),
            pl.BlockSpec((None, chunk, d_ssd), lambda bi, c: (bi, c, 0)),
            pl.BlockSpec((None, CONV_WIDTH - 1, ch), lambda bi, c: (bi, 0, 0)),
            pl.BlockSpec((None, SSD_GROUPS, gw, SSD_STATE), lambda bi, c: (bi, 0, 0, 0)),
            full((CONV_WIDTH, ch)), full((1, ch)), full((1, LANES)), full((1, LANES)), full((1, d_ssd)),
            full((LANES, d_ssd)), full((chunk, chunk)),
        ],
        out_specs=[
            pl.BlockSpec((None, chunk, d_ssd), lambda bi, c: (bi, c, 0)),
            pl.BlockSpec((None, SSD_GROUPS, gw, SSD_STATE), lambda bi, c: (bi, 0, 0, 0)),
        ],
        out_shape=[jax.ShapeDtypeStruct((b, t, d_ssd), BF16),
                   jax.ShapeDtypeStruct((b, SSD_GROUPS, gw, SSD_STATE), F32)],
        scratch_shapes=[pltpu.VMEM((chunk + SUBLANES, ch), F32), pltpu.VMEM((chunk, ch), F32),
                        pltpu.VMEM((SSD_GROUPS, SSD_STATE, gw), F32)],
        compiler_params=pltpu.CompilerParams(
            dimension_semantics=("arbitrary", "arbitrary"),
            vmem_limit_bytes=_vmem_limit(block_bytes, scratch_bytes, temp_bytes)),
        name="conv_ssd",
    )(xbc, dt_raw, z, conv_past, state_g, conv_w, conv_b.reshape(1, ch), dtb, alog, dskip, expand, tri)
    return y, state.reshape(b, heads, SSD_HEAD_DIM, SSD_STATE)


def _lagged_maps(n_rows, n_col):
    cur = lambda i, j: (jnp.minimum(i, n_rows - 1), 0)
    wcol = lambda i, j: (0, jnp.where(i < n_rows, j, n_col - 1))
    prev = lambda i, j: (jnp.maximum(i - 1, 0), jnp.where(i > 0, j, 0))
    ncol = lambda i, j: (0, jnp.where(i > 0, j, 0))
    return cur, wcol, prev, ncol


def _lagged_steps(i, n_rows, finish, matmul):
    if n_rows > 1:
        @pl.when((i > 0) & (i < n_rows))
        def _():
            finish()
            matmul()

    @pl.when(i == 0)
    def _():
        matmul()

    @pl.when(i == n_rows)
    def _():
        finish()


def _out_proj_kernel(usb_ref, ussd_ref, w_ref, x_ref, nw_ref, x1_ref, x1b_ref, m_ref, ssq_ref, inv_ref, ginv_ref,
                     *, n_rows, d_model, row_chunk):
    i = pl.program_id(0)
    j = pl.program_id(1)
    tm, d_sb = usb_ref.shape

    @pl.when(j == 0)
    def _():
        @pl.when(i > 0)
        def _():
            inv_ref[...] = lax.rsqrt(ssq_ref[...] * (1.0 / d_model) + EPS)

        ssq_ref[...] = jnp.zeros_like(ssq_ref)

        @pl.when(i < n_rows)
        def _():
            def body(r, carry):
                rows = pl.ds(pl.multiple_of(r * row_chunk, row_chunk), row_chunk)
                for g, u_ref in enumerate((usb_ref, ussd_ref)):
                    u = u_ref[rows, :].astype(F32)
                    ginv_ref[g, rows, :] = lax.rsqrt(jnp.mean(u * u, axis=-1, keepdims=True) + EPS)
                return carry

            lax.fori_loop(0, tm // row_chunk, body, 0, unroll=2)

    def finish():
        x1 = x_ref[...] + m_ref[j] * inv_ref[...] * nw_ref[...]
        x1_ref[...] = x1
        x1b_ref[...] = x1.astype(BF16)

    def matmul():
        mt = (_dot(usb_ref[...], w_ref[0:d_sb, :]) * ginv_ref[0]
              + _dot(ussd_ref[...], w_ref[d_sb:, :]) * ginv_ref[1])
        m_ref[j] = mt
        ssq_ref[...] += jnp.sum(mt * mt, axis=-1, keepdims=True)

    _lagged_steps(i, n_rows, finish, matmul)


def _out_proj(u_sb, u_ssd, w, x, norm_w, *, tm, tn):
    m, d_sb = u_sb.shape
    d_ssd = u_ssd.shape[1]
    d_mix, d = w.shape
    assert m % tm == 0 and d % tn == 0 and d_mix == d_sb + d_ssd
    n_col = d // tn
    n_rows = m // tm
    cur, wcol, prev, ncol = _lagged_maps(n_rows, n_col)
    block_bytes = (_nbytes((tm, d_mix), BF16) + _nbytes((d_mix, tn), BF16) + 2 * _nbytes((tm, tn), F32)
                   + _nbytes((tm, tn), BF16) + _nbytes((SUBLANES, tn), F32))
    scratch_bytes = _nbytes((tm, d), F32) + 4 * _nbytes((tm, LANES), F32)
    kern = functools.partial(_out_proj_kernel, n_rows=n_rows, d_model=d, row_chunk=32)
    return pl.pallas_call(
        kern,
        grid=(n_rows + 1, n_col),
        in_specs=[
            pl.BlockSpec((tm, d_sb), cur),
            pl.BlockSpec((tm, d_ssd), cur),
            pl.BlockSpec((d_mix, tn), wcol),
            pl.BlockSpec((tm, tn), prev),
            pl.BlockSpec((1, tn), ncol),
        ],
        out_specs=[pl.BlockSpec((tm, tn), prev), pl.BlockSpec((tm, tn), prev)],
        out_shape=[jax.ShapeDtypeStruct((m, d), F32), jax.ShapeDtypeStruct((m, d), BF16)],
        scratch_shapes=[pltpu.VMEM((n_col, tm, tn), F32), pltpu.VMEM((tm, 1), F32), pltpu.VMEM((tm, 1), F32),
                        pltpu.VMEM((2, tm, 1), F32)],
        compiler_params=pltpu.CompilerParams(
            dimension_semantics=("arbitrary", "arbitrary"),
            vmem_limit_bytes=_vmem_limit(block_bytes, scratch_bytes, 3 * _nbytes((tm, tn), F32))),
        name="out_proj",
    )(u_sb, u_ssd, w, x, norm_w.reshape(1, d))


def _ple_kernel(x1b_ref, wg_ref, p_ref, wp_ref, x1_ref, nw_ref, y_ref, ge_ref, ssq_ref, inv_ref, *, n_rows, d_model):
    i = pl.program_id(0)
    j = pl.program_id(1)

    @pl.when(j == 0)
    def _():
        @pl.when(i > 0)
        def _():
            inv_ref[...] = lax.rsqrt(ssq_ref[...] * (1.0 / d_model) + EPS)

        ssq_ref[...] = jnp.zeros_like(ssq_ref)

    def finish():
        y_ref[...] = x1_ref[...] + ge_ref[j] * inv_ref[...] * nw_ref[...]

    def matmul():
        gate = _sigmoid(_dot(x1b_ref[...], wg_ref[...]))
        emb = _dot(p_ref[...].astype(BF16), wp_ref[...])
        ge = gate * emb
        ge_ref[j] = ge
        ssq_ref[...] += jnp.sum(ge * ge, axis=-1, keepdims=True)

    _lagged_steps(i, n_rows, finish, matmul)


def _ple(x1b, w_gate, p, w_proj, x1, norm_w, *, tm, tn):
    m, d = x1.shape
    dp = p.shape[1]
    assert m % tm == 0 and d % tn == 0
    n_col = d // tn
    n_rows = m // tm
    cur, wcol, prev, ncol = _lagged_maps(n_rows, n_col)
    block_bytes = (_nbytes((tm, d), BF16) + _nbytes((d, tn), BF16) + _nbytes((tm, dp), F32) + _nbytes((dp, tn), BF16)
                   + 2 * _nbytes((tm, tn), F32) + _nbytes((SUBLANES, tn), F32))
    scratch_bytes = _nbytes((tm, d), F32) + 2 * _nbytes((tm, LANES), F32)
    kern = functools.partial(_ple_kernel, n_rows=n_rows, d_model=d)
    return pl.pallas_call(
        kern,
        grid=(n_rows + 1, n_col),
        in_specs=[
            pl.BlockSpec((tm, d), cur),
            pl.BlockSpec((d, tn), wcol),
            pl.BlockSpec((tm, dp), cur),
            pl.BlockSpec((dp, tn), wcol),
            pl.BlockSpec((tm, tn), prev),
            pl.BlockSpec((1, tn), ncol),
        ],
        out_specs=pl.BlockSpec((tm, tn), prev),
        out_shape=jax.ShapeDtypeStruct((m, d), F32),
        scratch_shapes=[pltpu.VMEM((n_col, tm, tn), F32), pltpu.VMEM((tm, 1), F32), pltpu.VMEM((tm, 1), F32)],
        compiler_params=pltpu.CompilerParams(
            dimension_semantics=("arbitrary", "arbitrary"),
            vmem_limit_bytes=_vmem_limit(block_bytes, scratch_bytes, 3 * _nbytes((tm, tn), F32))),
        name="ple",
    )(x1b, w_gate, p, w_proj, x1, norm_w.reshape(1, d))


def _row_tile(m, target):
    return target if m % target == 0 else m


def _trunk_layer(x, p, k_past, v_past, conv_past, ssd_past, wts, *, d_sb, d_ssd, conv_ch):
    (norm_pre, norm_post, w_in_t, w_dt_t, conv_w, conv_b, dt_bias, a_log, d_skip, w_out, w_gate, w_proj,
     ple_norm) = wts
    b, t, d = x.shape
    m = b * t
    xf = x.reshape(m, d)
    tm = _row_tile(m, 1024)
    h = _prenorm(xf, norm_pre.reshape(1, d), tm=_row_tile(m, 256))
    q, k, v, g, xbc, z, dt_raw = _in_proj(h, w_in_t, w_dt_t, d_sb=d_sb, conv_ch=conv_ch, d_ssd=d_ssd, tm=tm)
    tk = 256 if t % 256 == 0 else t
    heads = d_sb // SB_HEAD_DIM
    nh = 1 if k_past is None else heads
    u_sb = _attention(q.reshape(b, t, d_sb), k.reshape(b, t, d_sb), v.reshape(b, t, d_sb), g.reshape(b, t, d_sb),
                      k_past, v_past, tk=tk, tp=256, nh=nh)
    chunk = 128 if t % 128 == 0 else t
    u_ssd, ssd_new = _ssd(xbc.reshape(b, t, conv_ch), dt_raw.reshape(b, t, LANES), z.reshape(b, t, d_ssd),
                          conv_past, ssd_past, conv_w, conv_b, dt_bias, a_log, d_skip, d_ssd=d_ssd, chunk=chunk)
    x1, x1b = _out_proj(u_sb.reshape(m, d_sb), u_ssd.reshape(m, d_ssd), w_out, xf, norm_post,
                        tm=tm, tn=512 if d % 512 == 0 else d)
    yo = _ple(x1b, w_gate, p.reshape(m, p.shape[-1]), w_proj, x1, ple_norm, tm=tm, tn=512 if d % 512 == 0 else d)
    conv_new = xbc.reshape(b, t, conv_ch)[:, t - (CONV_WIDTH - 1):, :]
    return (yo.reshape(b, t, d), k.reshape(b, t, heads, SB_HEAD_DIM), v.reshape(b, t, heads, SB_HEAD_DIM),
            conv_new, ssd_new)


def kernel(x_prompt, x_sample, cache_k, cache_v, state_conv, state_ssd, p_prompt, p_sample, norm_pre, norm_post,
           w_in, conv_w, conv_b, dt_bias, a_log, d_skip, sb_norm, ssd_norm, w_out, w_ple_gate, w_ple_proj, ple_norm):
    depth = norm_pre.shape[0]
    ssd_heads = dt_bias.shape[1]
    d_ssd = ssd_heads * SSD_HEAD_DIM
    d_sb = sb_norm.shape[1]
    conv_ch = conv_w.shape[2]
    n_main = 4 * d_sb + conv_ch + d_ssd
    bp = x_prompt.shape[0]
    yp, ys = x_prompt, x_sample
    outs = [[] for _ in range(8)]
    for i in range(depth):
        w_in_t = jnp.swapaxes(w_in[i], 0, 1).astype(BF16)
        w_dt_t = jnp.pad(w_in_t[n_main:], ((0, LANES - ssd_heads), (0, 0)))
        group_gain = jnp.concatenate([sb_norm[i], ssd_norm[i]]).astype(F32)
        w_out_g = (group_gain[:, None] * w_out[i]).astype(BF16)
        wts = (norm_pre[i], norm_post[i], w_in_t, w_dt_t, conv_w[i], conv_b[i], dt_bias[i], a_log[i], d_skip[i],
               w_out_g, w_ple_gate[i].astype(BF16), w_ple_proj[i].astype(BF16), ple_norm[i])
        dims = dict(d_sb=d_sb, d_ssd=d_ssd, conv_ch=conv_ch)
        zero_conv = jnp.zeros((bp, CONV_WIDTH - 1, conv_ch), F32)
        zero_ssd = jnp.zeros((bp, ssd_heads, SSD_HEAD_DIM, SSD_STATE), F32)
        yp, kp, vp, cp, sp = _trunk_layer(yp, p_prompt[i], None, None, zero_conv, zero_ssd, wts, **dims)
        ys, ks, vs, cs, ss = _trunk_layer(ys, p_sample[i], cache_k[i], cache_v[i], state_conv[i], state_ssd[i],
                                          wts, **dims)
        for lst, val in zip(outs, (kp, vp, cp, sp, ks, vs, cs, ss)):
            lst.append(val)
    stacked = [jnp.stack(lst) for lst in outs]
    return (yp, ys, *stacked)
```

```python
import functools
import math

import jax
import jax.numpy as jnp
from jax import lax
from jax.experimental import pallas as pl
from jax.experimental.pallas import tpu as pltpu

F32 = jnp.float32
BF16 = jnp.bfloat16

EPS = 1e-6
SB_HEAD_DIM = 128
SSD_HEAD_DIM = 64
SSD_STATE = 128
SSD_GROUPS = 8
CONV_WIDTH = 4
LANES = 128
SUBLANES = 8
V7X_VMEM_BYTES = 64 * 2**20


def _vmem_limit(block_bytes, scratch_bytes, temp_bytes):
    need = 2 * block_bytes + scratch_bytes + temp_bytes
    return int(min(need, V7X_VMEM_BYTES - 2 * 2**20))


def _nbytes(shape, dtype):
    return math.prod(shape) * jnp.dtype(dtype).itemsize


def _sigmoid(x):
    return 1.0 / (1.0 + jnp.exp(-x))


def _silu(x):
    return x * _sigmoid(x)


def _split_bf16(x, parts):
    out = []
    r = x
    for p in range(parts):
        h = r.astype(BF16)
        out.append(h)
        if p + 1 < parts:
            r = r - h.astype(F32)
    return out


def _dot(a, b):
    return jnp.dot(a, b, preferred_element_type=F32)


def _dot_nt(a, b):
    return lax.dot_general(a, b, (((1,), (1,)), ((), ())), preferred_element_type=F32)


def _dot_split(x, m, parts):
    acc = None
    for h in _split_bf16(x, parts):
        t = _dot(h, m)
        acc = t if acc is None else acc + t
    return acc


def _in_proj_kernel(x_ref, nw_ref, w_ref, wdt_ref, q_ref, k_ref, v_ref, g_ref, xbc_ref, z_ref, dt_ref,
                    h_ref, hn_ref, *, bounds, q_scale, n_sub):
    i = pl.program_id(0)
    j = pl.program_id(1)
    rs = x_ref.shape[0]

    @pl.when((j == 0) & (i > 0))
    def _():
        h_ref[...] = hn_ref[...]

    def norm_rows():
        xf = x_ref[...]
        ms = jnp.mean(xf * xf, axis=-1, keepdims=True)
        rows = pl.ds(pl.multiple_of(j * rs, rs), rs)
        hn_ref[rows, :] = (xf * lax.rsqrt(ms + EPS) * nw_ref[...]).astype(BF16)

    @pl.when((i == 0) & (j < n_sub))
    def _():
        norm_rows()

    def segment(lo, hi, out_ref, scale, with_norm):
        if lo >= hi:
            return

        @pl.when((j >= lo) & (j < hi) & (i > 0))
        def _():
            acc = _dot_nt(h_ref[...], w_ref[...])
            if scale is not None:
                acc = acc * scale
            out_ref[...] = acc.astype(out_ref.dtype)
            if with_norm:
                norm_rows()

    outs = (q_ref, k_ref, v_ref, g_ref, xbc_ref, z_ref)
    for s, out_ref in enumerate(outs):
        lo, hi = bounds[s], bounds[s + 1]
        scale = q_scale if s == 0 else None
        segment(lo, min(hi, n_sub), out_ref, scale, True)
        segment(max(lo, n_sub), hi, out_ref, scale, False)

    @pl.when((j == bounds[-1]) & (i > 0))
    def _dt():
        dt_ref[...] = _dot_nt(h_ref[...], wdt_ref[...])


def _in_proj(x, norm_w, w_t, w_dt_t, *, d_sb, conv_ch, d_ssd, tm):
    m, d = x.shape
    tn = math.gcd(math.gcd(d_sb, 512), math.gcd(conv_ch, d_ssd))
    widths = (d_sb, d_sb, d_sb, d_sb, conv_ch, d_ssd)
    dtypes = (BF16, F32, F32, BF16, F32, BF16)
    bounds = [0]
    for w in widths:
        bounds.append(bounds[-1] + w // tn)
    bounds = tuple(bounds)
    n_tiles = bounds[-1]
    n_rows = m // tm
    n_sub = 8
    rs = tm // n_sub
    assert w_t.shape[1] == d and w_t.shape[0] >= n_tiles * tn and w_dt_t.shape == (LANES, d)
    assert m % tm == 0 and tm % n_sub == 0 and rs % 16 == 0 and n_sub <= n_tiles

    def out_map(lo, hi):
        return lambda i, j: (jnp.maximum(i - 1, 0), jnp.where(i > 0, jnp.clip(j - lo, 0, hi - lo - 1), 0))

    out_shapes = [jax.ShapeDtypeStruct((m, wd), dt) for wd, dt in zip(widths, dtypes)]
    out_shapes.append(jax.ShapeDtypeStruct((m, LANES), F32))
    out_specs = [pl.BlockSpec((tm, tn), out_map(bounds[s], bounds[s + 1])) for s in range(len(widths))]
    out_specs.append(pl.BlockSpec((tm, LANES), lambda i, j: (jnp.maximum(i - 1, 0), 0)))

    block_bytes = (_nbytes((rs, d), F32) + _nbytes((tn, d), BF16) + _nbytes((LANES, d), BF16)
                   + sum(_nbytes((tm, tn), dt) for dt in dtypes) + _nbytes((tm, LANES), F32))
    scratch_bytes = 2 * _nbytes((tm, d), BF16)
    temp_bytes = 2 * _nbytes((tm, tn), F32)
    kern = functools.partial(_in_proj_kernel, bounds=bounds, q_scale=SB_HEAD_DIM ** -0.5 * math.log2(math.e),
                             n_sub=n_sub)
    return pl.pallas_call(
        kern,
        grid=(n_rows + 1, n_tiles + 1),
        in_specs=[
            pl.BlockSpec((rs, d), lambda i, j: (jnp.minimum(i, n_rows - 1) * n_sub + jnp.minimum(j, n_sub - 1), 0)),
            pl.BlockSpec((1, d), lambda i, j: (0, 0)),
            pl.BlockSpec((tn, d), lambda i, j: (jnp.where(i > 0, jnp.minimum(j, n_tiles - 1), 0), 0)),
            pl.BlockSpec((LANES, d), lambda i, j: (0, 0)),
        ],
        out_specs=out_specs,
        out_shape=out_shapes,
        scratch_shapes=[pltpu.VMEM((tm, d), BF16), pltpu.VMEM((tm, d), BF16)],
        compiler_params=pltpu.CompilerParams(
            dimension_semantics=("arbitrary", "arbitrary"),
            vmem_limit_bytes=_vmem_limit(block_bytes, scratch_bytes, temp_bytes)),
        name="in_proj",
    )(x, norm_w, w_t, w_dt_t)


def _attn_kernel(*refs, t, tk, tp, n_past, nh):
    if n_past:
        q_ref, k_ref, v_ref, g_ref, kp_ref, vp_ref, u_ref, o_ref, acc_ref, carry_ref = refs
    else:
        q_ref, k_ref, v_ref, g_ref, u_ref, o_ref, acc_ref, carry_ref = refs
    d = SB_HEAD_DIM
    row = lax.broadcasted_iota(jnp.int32, (tk, tk), 0)
    col = lax.broadcasted_iota(jnp.int32, (tk, tk), 1)
    causal = col < row

    def strip(q, kb, vb, u, r0, n_diag, hs, n):
        z = _dot_nt(q, kb.astype(BF16))
        sp = jnp.log2(1.0 + jnp.exp2(-jnp.abs(z)))
        log_stop = jnp.minimum(z, 0.0) - sp
        log_stay = log_stop - z
        n_rows = z.shape[0]
        has_rest = n_rows > n_diag
        if n_diag:
            diag_stay = jnp.where(causal, log_stay[:n_diag], 0.0)
            log_stay = jnp.concatenate([diag_stay, log_stay[n_diag:]], axis=0) if has_rest else diag_stay
        later = _dot(log_stay.astype(BF16), u)
        total = jnp.sum(log_stay, axis=-1, keepdims=True)
        parts = []
        if n_diag:
            att_diag = jnp.where(causal, jnp.exp2(log_stop[:n_diag] + later[:n_diag]), 0.0)
            parts.append(att_diag)
        if has_rest:
            carry = carry_ref[n, r0 + n_diag:r0 + n_rows, :]
            parts.append(jnp.exp2(log_stop[n_diag:] + later[n_diag:] + carry))
        att = jnp.concatenate(parts, axis=0) if len(parts) > 1 else parts[0]
        pv = _dot(att.astype(BF16), vb.astype(BF16))
        if n_diag:
            acc_ref[r0:r0 + n_diag, hs] = pv[:n_diag]
            carry_ref[n, r0:r0 + n_diag, :] = total[:n_diag]
        if has_rest:
            acc_ref[r0 + n_diag:r0 + n_rows, hs] += pv[n_diag:]
            carry_ref[n, r0 + n_diag:r0 + n_rows, :] = carry + total[n_diag:]

    u_new = u_ref[:tk, :tk]
    for c in reversed(range(t // tk)):
        r0 = c * tk
        for n in range(nh):
            hs = slice(n * d, (n + 1) * d)
            strip(q_ref[r0:, hs], k_ref[r0:r0 + tk, hs], v_ref[r0:r0 + tk, hs], u_new, r0, tk, hs, n)
    if n_past:
        u_past = u_ref[:tp, :tp]
        for c in reversed(range(n_past)):
            kt = jnp.swapaxes(kp_ref[c * tp:(c + 1) * tp], 0, 1)
            vt = jnp.swapaxes(vp_ref[c * tp:(c + 1) * tp], 0, 1)
            for n in range(nh):
                hs = slice(n * d, (n + 1) * d)
                strip(q_ref[:, hs], kt[n], vt[n], u_past, 0, 0, hs, n)
    o_ref[...] = (acc_ref[...] * _silu(g_ref[...].astype(F32))).astype(o_ref.dtype)


def _attention(q, k, v, g, k_past, v_past, *, tk, tp, nh):
    b, t, hd = k.shape
    d = SB_HEAD_DIM
    heads = hd // d
    p = 0 if k_past is None else k_past.shape[1]
    assert t % tk == 0 and p % tp == 0 and heads % nh == 0 and (p == 0 or nh == heads)
    n_past = p // tp
    tu = max(tk, tp) if n_past else tk
    ridx = lax.broadcasted_iota(jnp.int32, (tu, tu), 0)
    cidx = lax.broadcasted_iota(jnp.int32, (tu, tu), 1)
    u = (ridx > cidx).astype(BF16)
    w = nh * d

    head_spec = pl.BlockSpec((None, t, w), lambda bi, h: (bi, 0, h))
    in_specs = [head_spec, head_spec, head_spec, head_spec]
    args = [q, k, v, g]
    block_bytes = 3 * _nbytes((t, w), BF16) + 2 * _nbytes((t, w), F32) + _nbytes((tu, tu), BF16)
    if n_past:
        past_spec = pl.BlockSpec((None, p, heads, d), lambda bi, h: (bi, 0, 0, 0))
        in_specs += [past_spec, past_spec]
        args += [k_past, v_past]
        block_bytes += 2 * _nbytes((p, w), F32)
    in_specs.append(pl.BlockSpec((tu, tu), lambda bi, h: (0, 0)))
    args.append(u)
    scratch_bytes = _nbytes((t, w), F32) + nh * _nbytes((t, LANES), F32)
    temp_bytes = 8 * _nbytes((t, tu), F32)
    kern = functools.partial(_attn_kernel, t=t, tk=tk, tp=tp, n_past=n_past, nh=nh)
    return pl.pallas_call(
        kern,
        grid=(b, heads // nh),
        in_specs=in_specs,
        out_specs=head_spec,
        out_shape=jax.ShapeDtypeStruct((b, t, hd), BF16),
        scratch_shapes=[pltpu.VMEM((t, w), F32), pltpu.VMEM((nh, t, 1), F32)],
        compiler_params=pltpu.CompilerParams(
            dimension_semantics=("arbitrary", "arbitrary"),
            vmem_limit_bytes=_vmem_limit(block_bytes, scratch_bytes, temp_bytes)),
        name="stick_breaking",
    )(*args)


def _ssd_kernel(xbc_ref, dt_ref, z_ref, convp_ref, statep_ref, cw_ref, cb_ref, dtb_ref, alog_ref, dskip_ref, e_ref,
                tri_ref, y_ref, state_ref, ext_ref, act_ref, ht_ref, *, chunk, d_ssd, col_slab):
    c = pl.program_id(1)
    n_chunks = pl.num_programs(1)
    ch = xbc_ref.shape[-1]
    gn = SSD_GROUPS * SSD_STATE
    hpg = d_ssd // SSD_HEAD_DIM // SSD_GROUPS
    gw = hpg * SSD_HEAD_DIM
    pad = SUBLANES

    @pl.when(c == 0)
    def _init():
        ext_ref[0:pad, :] = jnp.zeros((pad, ch), F32)
        ext_ref[pad - (CONV_WIDTH - 1):pad, :] = convp_ref[...]
        for g in range(SSD_GROUPS):
            ht_ref[g] = statep_ref[g].T

    @pl.when(c > 0)
    def _shift():
        ext_ref[0:pad, :] = ext_ref[chunk:chunk + pad, :]

    ext_ref[pad:pad + chunk, :] = xbc_ref[...]

    for s in range(ch // col_slab):
        cols = slice(s * col_slab, (s + 1) * col_slab)
        acc = cb_ref[:, cols]
        for jj in range(CONV_WIDTH):
            lo = pad - (CONV_WIDTH - 1) + jj
            acc = acc + cw_ref[jj:jj + 1, cols] * ext_ref[lo:lo + chunk, cols]
        act_ref[:, cols] = _silu(acc)

    dt_in = dt_ref[...] + dtb_ref[...]
    dt = jnp.maximum(dt_in, 0.0) + jnp.log(1.0 + jnp.exp(-jnp.abs(dt_in)))
    da = dt * (-jnp.exp(alog_ref[...]))
    tri = tri_ref[...]
    acum = _dot_tri(tri, da)
    last = acum[chunk - 1:chunk, :]
    acum_t = acum.T
    dt_t = dt.T
    row = lax.broadcasted_iota(jnp.int32, (chunk, chunk), 0)
    col = lax.broadcasted_iota(jnp.int32, (chunk, chunk), 1)
    causal = col <= row

    e = e_ref[...]
    de_e = _dot_split(jnp.exp(last - acum) * dt, e, 2)
    ea_e = _dot_split(jnp.exp(acum), e, 2)
    bd_e = _dot_split(jnp.broadcast_to(jnp.exp(last), (SUBLANES, LANES)), e, 2)[0:1, :]

    lane = lax.broadcasted_iota(jnp.int32, (chunk, gw), 1)
    for g in range(SSD_GROUPS):
        gs = slice(g * gw, (g + 1) * gw)
        xg = act_ref[:, gs]
        bg = act_ref[:, d_ssd + g * SSD_STATE:d_ssd + (g + 1) * SSD_STATE].astype(BF16)
        cg = act_ref[:, d_ssd + gn + g * SSD_STATE:d_ssd + gn + (g + 1) * SSD_STATE].astype(BF16)
        cb = lax.dot_general(cg, bg, (((1,), (1,)), ((), ())), preferred_element_type=F32)
        xgb = xg.astype(BF16)
        y_intra = None
        for hh in range(hpg):
            h = g * hpg + hh
            diff = acum[:, h:h + 1] - acum_t[h:h + 1, :]
            decay = jnp.exp(jnp.where(causal, diff, -jnp.inf))
            w = (cb * decay * dt_t[h:h + 1, :]).astype(BF16)
            in_head = (lane >= hh * SSD_HEAD_DIM) & (lane < (hh + 1) * SSD_HEAD_DIM)
            t = _dot(w, jnp.where(in_head, xgb, jnp.zeros_like(xgb)))
            y_intra = t if y_intra is None else y_intra + t
        h_prev = ht_ref[g]
        y_inter = _dot(cg, h_prev.astype(BF16)) * ea_e[:, gs]
        xw = (xg * de_e[:, gs]).astype(BF16)
        s_t = lax.dot_general(bg, xw, (((0,), (0,)), ((), ())), preferred_element_type=F32)
        ht_ref[g] = h_prev * bd_e[:, gs] + s_t
        y = y_intra + y_inter + dskip_ref[:, gs] * xg
        y_ref[:, gs] = (y * _silu(z_ref[:, gs].astype(F32))).astype(y_ref.dtype)

    @pl.when(c == n_chunks - 1)
    def _final():
        for g in range(SSD_GROUPS):
            state_ref[g] = ht_ref[g].T


def _dot_tri(tri, x):
    acc = None
    for h in _split_bf16(x, 3):
        t = _dot(tri, h)
        acc = t if acc is None else acc + t
    return acc


def _ssd(xbc, dt_raw, z, conv_past, state_past, conv_w, conv_b, dt_bias, a_log, d_skip, *, d_ssd, chunk):
    b, t, ch = xbc.shape
    heads = d_ssd // SSD_HEAD_DIM
    hpg = heads // SSD_GROUPS
    gw = hpg * SSD_HEAD_DIM
    assert t % chunk == 0 and heads <= LANES
    n_chunks = t // chunk
    pad_h = LANES - heads
    dtb = jnp.pad(dt_bias.astype(F32), (0, pad_h)).reshape(1, LANES)
    alog = jnp.pad(a_log.astype(F32), (0, pad_h)).reshape(1, LANES)
    dskip = jnp.repeat(d_skip.astype(F32), SSD_HEAD_DIM).reshape(1, d_ssd)
    hidx = lax.broadcasted_iota(jnp.int32, (LANES, d_ssd), 0)
    lidx = lax.broadcasted_iota(jnp.int32, (LANES, d_ssd), 1)
    expand = (lidx // SSD_HEAD_DIM == hidx).astype(BF16)
    ridx = lax.broadcasted_iota(jnp.int32, (chunk, chunk), 0)
    cidx = lax.broadcasted_iota(jnp.int32, (chunk, chunk), 1)
    tri = (cidx <= ridx).astype(BF16)
    state_g = state_past.reshape(b, SSD_GROUPS, gw, SSD_STATE)

    full = lambda shape: pl.BlockSpec(shape, lambda bi, c: (0,) * len(shape))
    col_slab = 512
    block_bytes = (_nbytes((chunk, ch), F32) + _nbytes((chunk, LANES), F32) + _nbytes((SUBLANES, ch), F32)
                   + 2 * _nbytes((SSD_GROUPS, gw, SSD_STATE), F32) + _nbytes((CONV_WIDTH + 1, ch), F32) * 2
                   + _nbytes((LANES, d_ssd), BF16) + _nbytes((chunk, chunk), BF16)
                   + 2 * _nbytes((chunk, d_ssd), BF16))
    scratch_bytes = (_nbytes((chunk + SUBLANES, ch), F32) + _nbytes((chunk, ch), F32)
                     + _nbytes((SSD_GROUPS, SSD_STATE, gw), F32))
    temp_bytes = 4 * _nbytes((chunk, d_ssd), F32) + 16 * _nbytes((chunk, max(chunk, LANES)), F32)
    kern = functools.partial(_ssd_kernel, chunk=chunk, d_ssd=d_ssd, col_slab=col_slab)
    y, state = pl.pallas_call(
        kern,
        grid=(b, n_chunks),
        in_specs=[
            pl.BlockSpec((None, chunk, ch), lambda bi, c: (bi, c, 0)),
            pl.BlockSpec((None, chunk, LANES), lambda bi, c: (bi, c, 0)),
            pl.BlockSpec((None, chunk, d_ssd), lambda bi, c: (bi, c, 0)),
            pl.BlockSpec((None, CONV_WIDTH - 1, ch), lambda bi, c: (bi, 0, 0)),
            pl.BlockSpec((None, SSD_GROUPS, gw, SSD_STATE), lambda bi, c: (bi, 0, 0, 0)),
            full((CONV_WIDTH, ch)), full((1, ch)), full((1, LANES)), full((1, LANES)), full((1, d_ssd)),
            full((LANES, d_ssd)), full((chunk, chunk)),
        ],
        out_specs=[
            pl.BlockSpec((None, chunk, d_ssd), lambda bi, c: (bi, c, 0)),
            pl.BlockSpec((None, SSD_GROUPS, gw, SSD_STATE), lambda bi, c: (bi, 0, 0, 0)),
        ],
        out_shape=[jax.ShapeDtypeStruct((b, t, d_ssd), BF16),
                   jax.ShapeDtypeStruct((b, SSD_GROUPS, gw, SSD_STATE), F32)],
        scratch_shapes=[pltpu.VMEM((chunk + SUBLANES, ch), F32), pltpu.VMEM((chunk, ch), F32),
                        pltpu.VMEM((SSD_GROUPS, SSD_STATE, gw), F32)],
        compiler_params=pltpu.CompilerParams(
            dimension_semantics=("arbitrary", "arbitrary"),
            vmem_limit_bytes=_vmem_limit(block_bytes, scratch_bytes, temp_bytes)),
        name="conv_ssd",
    )(xbc, dt_raw, z, conv_past, state_g, conv_w, conv_b.reshape(1, ch), dtb, alog, dskip, expand, tri)
    return y, state.reshape(b, heads, SSD_HEAD_DIM, SSD_STATE)


def _lagged_maps(n_rows, n_col):
    cur = lambda i, j: (jnp.minimum(i, n_rows - 1), 0)
    wcol = lambda i, j: (0, jnp.where(i < n_rows, j, n_col - 1))
    prev = lambda i, j: (jnp.maximum(i - 1, 0), jnp.where(i > 0, j, 0))
    ncol = lambda i, j: (0, jnp.where(i > 0, j, 0))
    return cur, wcol, prev, ncol


def _lagged_steps(i, n_rows, finish, matmul):
    if n_rows > 1:
        @pl.when((i > 0) & (i < n_rows))
        def _():
            finish()
            matmul()

    @pl.when(i == 0)
    def _():
        matmul()

    @pl.when(i == n_rows)
    def _():
        finish()


def _out_proj_kernel(usb_ref, ussd_ref, w_ref, x_ref, nw_ref, x1_ref, x1b_ref, m_ref, ssq_ref, inv_ref, ginv_ref,
                     *, n_rows, d_model, row_chunk):
    i = pl.program_id(0)
    j = pl.program_id(1)
    tm, d_sb = usb_ref.shape

    @pl.when(j == 0)
    def _():
        @pl.when(i > 0)
        def _():
            inv_ref[...] = lax.rsqrt(ssq_ref[...] * (1.0 / d_model) + EPS)

        ssq_ref[...] = jnp.zeros_like(ssq_ref)

        @pl.when(i < n_rows)
        def _():
            def body(r, carry):
                rows = pl.ds(pl.multiple_of(r * row_chunk, row_chunk), row_chunk)
                for g, u_ref in enumerate((usb_ref, ussd_ref)):
                    u = u_ref[rows, :].astype(F32)
                    ginv_ref[g, rows, :] = lax.rsqrt(jnp.mean(u * u, axis=-1, keepdims=True) + EPS)
                return carry

            lax.fori_loop(0, tm // row_chunk, body, 0, unroll=2)

    def finish():
        x1 = x_ref[...] + m_ref[j] * inv_ref[...] * nw_ref[...]
        x1_ref[...] = x1
        x1b_ref[...] = x1.astype(BF16)

    def matmul():
        mt = (_dot(usb_ref[...], w_ref[0:d_sb, :]) * ginv_ref[0]
              + _dot(ussd_ref[...], w_ref[d_sb:, :]) * ginv_ref[1])
        m_ref[j] = mt
        ssq_ref[...] += jnp.sum(mt * mt, axis=-1, keepdims=True)

    _lagged_steps(i, n_rows, finish, matmul)


def _out_proj(u_sb, u_ssd, w, x, norm_w, *, tm, tn):
    m, d_sb = u_sb.shape
    d_ssd = u_ssd.shape[1]
    d_mix, d = w.shape
    assert m % tm == 0 and d % tn == 0 and d_mix == d_sb + d_ssd
    n_col = d // tn
    n_rows = m // tm
    cur, wcol, prev, ncol = _lagged_maps(n_rows, n_col)
    block_bytes = (_nbytes((tm, d_mix), BF16) + _nbytes((d_mix, tn), BF16) + 2 * _nbytes((tm, tn), F32)
                   + _nbytes((tm, tn), BF16) + _nbytes((SUBLANES, tn), F32))
    scratch_bytes = _nbytes((tm, d), F32) + 4 * _nbytes((tm, LANES), F32)
    kern = functools.partial(_out_proj_kernel, n_rows=n_rows, d_model=d, row_chunk=32)
    return pl.pallas_call(
        kern,
        grid=(n_rows + 1, n_col),
        in_specs=[
            pl.BlockSpec((tm, d_sb), cur),
            pl.BlockSpec((tm, d_ssd), cur),
            pl.BlockSpec((d_mix, tn), wcol),
            pl.BlockSpec((tm, tn), prev),
            pl.BlockSpec((1, tn), ncol),
        ],
        out_specs=[pl.BlockSpec((tm, tn), prev), pl.BlockSpec((tm, tn), prev)],
        out_shape=[jax.ShapeDtypeStruct((m, d), F32), jax.ShapeDtypeStruct((m, d), BF16)],
        scratch_shapes=[pltpu.VMEM((n_col, tm, tn), F32), pltpu.VMEM((tm, 1), F32), pltpu.VMEM((tm, 1), F32),
                        pltpu.VMEM((2, tm, 1), F32)],
        compiler_params=pltpu.CompilerParams(
            dimension_semantics=("arbitrary", "arbitrary"),
            vmem_limit_bytes=_vmem_limit(block_bytes, scratch_bytes, 3 * _nbytes((tm, tn), F32))),
        name="out_proj",
    )(u_sb, u_ssd, w, x, norm_w.reshape(1, d))


def _ple_kernel(x1b_ref, wg_ref, p_ref, wp_ref, x1_ref, nw_ref, y_ref, ge_ref, ssq_ref, inv_ref, *, n_rows, d_model):
    i = pl.program_id(0)
    j = pl.program_id(1)

    @pl.when(j == 0)
    def _():
        @pl.when(i > 0)
        def _():
            inv_ref[...] = lax.rsqrt(ssq_ref[...] * (1.0 / d_model) + EPS)

        ssq_ref[...] = jnp.zeros_like(ssq_ref)

    def finish():
        y_ref[...] = x1_ref[...] + ge_ref[j] * inv_ref[...] * nw_ref[...]

    def matmul():
        gate = _sigmoid(_dot(x1b_ref[...], wg_ref[...]))
        emb = _dot(p_ref[...].astype(BF16), wp_ref[...])
        ge = gate * emb
        ge_ref[j] = ge
        ssq_ref[...] += jnp.sum(ge * ge, axis=-1, keepdims=True)

    _lagged_steps(i, n_rows, finish, matmul)


def _ple(x1b, w_gate, p, w_proj, x1, norm_w, *, tm, tn):
    m, d = x1.shape
    dp = p.shape[1]
    assert m % tm == 0 and d % tn == 0
    n_col = d // tn
    n_rows = m // tm
    cur, wcol, prev, ncol = _lagged_maps(n_rows, n_col)
    block_bytes = (_nbytes((tm, d), BF16) + _nbytes((d, tn), BF16) + _nbytes((tm, dp), F32) + _nbytes((dp, tn), BF16)
                   + 2 * _nbytes((tm, tn), F32) + _nbytes((SUBLANES, tn), F32))
    scratch_bytes = _nbytes((tm, d), F32) + 2 * _nbytes((tm, LANES), F32)
    kern = functools.partial(_ple_kernel, n_rows=n_rows, d_model=d)
    return pl.pallas_call(
        kern,
        grid=(n_rows + 1, n_col),
        in_specs=[
            pl.BlockSpec((tm, d), cur),
            pl.BlockSpec((d, tn), wcol),
            pl.BlockSpec((tm, dp), cur),
            pl.BlockSpec((dp, tn), wcol),
            pl.BlockSpec((tm, tn), prev),
            pl.BlockSpec((1, tn), ncol),
        ],
        out_specs=pl.BlockSpec((tm, tn), prev),
        out_shape=jax.ShapeDtypeStruct((m, d), F32),
        scratch_shapes=[pltpu.VMEM((n_col, tm, tn), F32), pltpu.VMEM((tm, 1), F32), pltpu.VMEM((tm, 1), F32)],
        compiler_params=pltpu.CompilerParams(
            dimension_semantics=("arbitrary", "arbitrary"),
            vmem_limit_bytes=_vmem_limit(block_bytes, scratch_bytes, 3 * _nbytes((tm, tn), F32))),
        name="ple",
    )(x1b, w_gate, p, w_proj, x1, norm_w.reshape(1, d))


def _row_tile(m, target):
    return target if m % target == 0 else m


def _trunk_layer(x, p, k_past, v_past, conv_past, ssd_past, wts, *, d_sb, d_ssd, conv_ch):
    (norm_pre, norm_post, w_in_t, w_dt_t, conv_w, conv_b, dt_bias, a_log, d_skip, w_out, w_gate, w_proj,
     ple_norm) = wts
    b, t, d = x.shape
    m = b * t
    xf = x.reshape(m, d)
    tm = _row_tile(m, 1024)
    q, k, v, g, xbc, z, dt_raw = _in_proj(xf, norm_pre.reshape(1, d), w_in_t, w_dt_t,
                                          d_sb=d_sb, conv_ch=conv_ch, d_ssd=d_ssd, tm=tm)
    tk = 256 if t % 256 == 0 else t
    heads = d_sb // SB_HEAD_DIM
    nh = 1 if k_past is None else heads
    u_sb = _attention(q.reshape(b, t, d_sb), k.reshape(b, t, d_sb), v.reshape(b, t, d_sb), g.reshape(b, t, d_sb),
                      k_past, v_past, tk=tk, tp=256, nh=nh)
    chunk = 128 if t % 128 == 0 else t
    u_ssd, ssd_new = _ssd(xbc.reshape(b, t, conv_ch), dt_raw.reshape(b, t, LANES), z.reshape(b, t, d_ssd),
                          conv_past, ssd_past, conv_w, conv_b, dt_bias, a_log, d_skip, d_ssd=d_ssd, chunk=chunk)
    x1, x1b = _out_proj(u_sb.reshape(m, d_sb), u_ssd.reshape(m, d_ssd), w_out, xf, norm_post,
                        tm=tm, tn=512 if d % 512 == 0 else d)
    yo = _ple(x1b, w_gate, p.reshape(m, p.shape[-1]), w_proj, x1, ple_norm, tm=tm, tn=512 if d % 512 == 0 else d)
    conv_new = xbc.reshape(b, t, conv_ch)[:, t - (CONV_WIDTH - 1):, :]
    return (yo.reshape(b, t, d), k.reshape(b, t, heads, SB_HEAD_DIM), v.reshape(b, t, heads, SB_HEAD_DIM),
            conv_new, ssd_new)


def kernel(x_prompt, x_sample, cache_k, cache_v, state_conv, state_ssd, p_prompt, p_sample, norm_pre, norm_post,
           w_in, conv_w, conv_b, dt_bias, a_log, d_skip, sb_norm, ssd_norm, w_out, w_ple_gate, w_ple_proj, ple_norm):
    depth = norm_pre.shape[0]
    ssd_heads = dt_bias.shape[1]
    d_ssd = ssd_heads * SSD_HEAD_DIM
    d_sb = sb_norm.shape[1]
    conv_ch = conv_w.shape[2]
    n_main = 4 * d_sb + conv_ch + d_ssd
    bp = x_prompt.shape[0]
    yp, ys = x_prompt, x_sample
    outs = [[] for _ in range(8)]
    for i in range(depth):
        w_in_t = jnp.swapaxes(w_in[i], 0, 1).astype(BF16)
        w_dt_t = jnp.pad(w_in_t[n_main:], ((0, LANES - ssd_heads), (0, 0)))
        group_gain = jnp.concatenate([sb_norm[i], ssd_norm[i]]).astype(F32)
        w_out_g = (group_gain[:, None] * w_out[i]).astype(BF16)
        wts = (norm_pre[i], norm_post[i], w_in_t, w_dt_t, conv_w[i], conv_b[i], dt_bias[i], a_log[i], d_skip[i],
               w_out_g, w_ple_gate[i].astype(BF16), w_ple_proj[i].astype(BF16), ple_norm[i])
        dims = dict(d_sb=d_sb, d_ssd=d_ssd, conv_ch=conv_ch)
        zero_conv = jnp.zeros((bp, CONV_WIDTH - 1, conv_ch), F32)
        zero_ssd = jnp.zeros((bp, ssd_heads, SSD_HEAD_DIM, SSD_STATE), F32)
        yp, kp, vp, cp, sp = _trunk_layer(yp, p_prompt[i], None, None, zero_conv, zero_ssd, wts, **dims)
        ys, ks, vs, cs, ss = _trunk_layer(ys, p_sample[i], cache_k[i], cache_v[i], state_conv[i], state_ssd[i],
                                          wts, **dims)
        for lst, val in zip(outs, (kp, vp, cp, sp, ks, vs, cs, ss)):
            lst.append(val)
    stacked = [jnp.stack(lst) for lst in outs]
    return (yp, ys, *stacked)
```
